```python
import math
import jax, jax.numpy as jnp
from jax import lax
import numpy as np

D_MODEL = 2048
BATCH = 2
SEQ = 4096
DEPTH = 1

CTX_LEN = 256
GRID_W = 64
MIX_WIDTH = D_MODEL
ATTN_WIDTH = MIX_WIDTH // 2
CONV_WIDTH = MIX_WIDTH - ATTN_WIDTH
HEAD_DIM = 128
N_HEADS = ATTN_WIDTH // HEAD_DIM
N_KV_HEADS = 2
GROUP = N_HEADS // N_KV_HEADS
KV_WIDTH = N_KV_HEADS * HEAD_DIM
CONV_GROUPS = CONV_WIDTH // HEAD_DIM
CONV_K = 3
Q_BLOCK = 128
ROPE_THETA = 10000.0
EPS = 1e-6
IN_COLS = ATTN_WIDTH + 2 * KV_WIDTH + ATTN_WIDTH + 4 * CONV_WIDTH

kernel_name = "hybrid_gqa_shortconv_dit_layer"


def rms_norm(x, g):
    xf = x.astype(jnp.float32)
    y = xf * lax.rsqrt(jnp.mean(xf * xf, axis=-1, keepdims=True) + EPS)
    return (y * g.astype(jnp.float32)).astype(x.dtype)


def adaln_params(cond, w_mod, b_mod):
    m = jax.nn.silu(cond) @ w_mod + b_mod
    return jnp.split(m, 3, axis=-1)


def axial_rope(x, row, col):
    half = HEAD_DIM // 2
    quarter = half // 2
    inv = ROPE_THETA ** (-jnp.arange(quarter, dtype=jnp.float32) / quarter)

    def rot(xa, pos):
        ang = pos.astype(jnp.float32)[:, None] * inv[None, :]
        cos = jnp.cos(ang)[None, :, None, :]
        sin = jnp.sin(ang)[None, :, None, :]
        xa = xa.astype(jnp.float32)
        x1, x2 = xa[..., :quarter], xa[..., quarter:]
        return jnp.concatenate([x1 * cos - x2 * sin, x2 * cos + x1 * sin], axis=-1)

    out = jnp.concatenate([rot(x[..., :half], row), rot(x[..., half:], col)], axis=-1)
    return out.astype(x.dtype)


def short_conv(u, w):
    L = u.shape[1]
    up = jnp.pad(u, ((0, 0), (1, 1), (0, 0)))
    return up[:, :L] * w[0] + up[:, 1:L + 1] * w[1] + up[:, 2:] * w[2]


def split_proj(p):
    sizes = [ATTN_WIDTH, KV_WIDTH, KV_WIDTH, ATTN_WIDTH,
             CONV_WIDTH, CONV_WIDTH, CONV_WIDTH, CONV_WIDTH]
    idx = np.cumsum(sizes)[:-1].tolist()
    return jnp.split(p, idx, axis=-1)


def qk_heads(q, k, v, q_g, k_g):
    B, L = q.shape[:2]
    q = rms_norm(q.reshape(B, L, N_HEADS, HEAD_DIM), q_g)
    k = rms_norm(k.reshape(B, L, N_KV_HEADS, HEAD_DIM), k_g)
    v = v.reshape(B, L, N_KV_HEADS, HEAD_DIM)
    return q, k, v


def latent_attention(q, k_lat, v_lat, k_ctx, v_ctx):
    B, S = q.shape[:2]
    scale = 1.0 / math.sqrt(HEAD_DIM)
    k_all = jnp.concatenate([k_ctx, k_lat], axis=1)
    v_all = jnp.concatenate([v_ctx, v_lat], axis=1)
    n_blk = S // Q_BLOCK
    qb = q.reshape(B, n_blk, Q_BLOCK, N_KV_HEADS, GROUP, HEAD_DIM).transpose(1, 0, 3, 4, 2, 5)

    def one_block(q_blk):
        s = jnp.einsum('bkgqd,bskd->bkgqs', q_blk, k_all).astype(jnp.float32) * scale
        p = jax.nn.softmax(s, axis=-1).astype(v_all.dtype)
        return jnp.einsum('bkgqs,bskd->bkgqd', p, v_all)

    o = lax.map(one_block, qb)
    return o.transpose(1, 0, 4, 2, 3, 5).reshape(B, S, ATTN_WIDTH)


def context_attention(q, k, v):
    B, L = q.shape[:2]
    scale = 1.0 / math.sqrt(HEAD_DIM)
    qg = q.reshape(B, L, N_KV_HEADS, GROUP, HEAD_DIM)
    s = jnp.einsum('bqkgd,bskd->bkgqs', qg, k).astype(jnp.float32) * scale
    p = jax.nn.softmax(s, axis=-1).astype(v.dtype)
    o = jnp.einsum('bkgqs,bskd->bqkgd', p, v)
    return o.reshape(B, L, ATTN_WIDTH)


def conv_branch(b, cg, h, gate_c, w):
    return jax.nn.silu(gate_c) * (b * short_conv(cg * h, w))


def setup_inputs(seed: int = 0) -> dict:
    key = jax.random.key(seed)
    ks = jax.random.split(key, 13)
    D = D_MODEL
    x = jax.random.normal(ks[0], (BATCH, SEQ, D), jnp.float32)
    c = jax.random.normal(ks[1], (BATCH, D), jnp.float32)
    ctx = jax.random.normal(ks[2], (BATCH, CTX_LEN, D), jnp.float32)
    c_ctx = 0.5 * jax.random.normal(ks[3], (D,), jnp.float32)
    w_mod = 0.3 * D ** -0.5 * jax.random.normal(ks[4], (DEPTH, D, 3 * D), jnp.float32)
    b_mod = 0.02 * jax.random.normal(ks[5], (DEPTH, 3 * D), jnp.float32)
    norm_g = 1.0 + 0.02 * jax.random.normal(ks[6], (DEPTH, D), jnp.float32)
    w_in = D ** -0.5 * jax.random.normal(ks[7], (DEPTH, D, IN_COLS), jnp.float32)
    q_norm_g = 1.0 + 0.02 * jax.random.normal(ks[8], (DEPTH, HEAD_DIM), jnp.float32)
    k_norm_g = 1.0 + 0.02 * jax.random.normal(ks[9], (DEPTH, HEAD_DIM), jnp.float32)
    conv_w = CONV_K ** -0.5 * jax.random.normal(ks[10], (DEPTH, CONV_K, CONV_WIDTH), jnp.float32)
    w_out = MIX_WIDTH ** -0.5 * jax.random.normal(ks[11], (DEPTH, MIX_WIDTH, D), jnp.float32)
    final_norm_g = 1.0 + 0.02 * jax.random.normal(ks[12], (D,), jnp.float32)
    return {"x": x, "c": c, "ctx": ctx, "c_ctx": c_ctx, "w_mod": w_mod, "b_mod": b_mod,
            "norm_g": norm_g, "w_in": w_in, "q_norm_g": q_norm_g, "k_norm_g": k_norm_g,
            "conv_w": conv_w, "w_out": w_out, "final_norm_g": final_norm_g}


def reference(x, c, ctx, c_ctx, w_mod, b_mod, norm_g, w_in, q_norm_g, k_norm_g,
              conv_w, w_out, final_norm_g):
    S = x.shape[1]
    ROWS = S // GRID_W
    row = jnp.repeat(jnp.arange(ROWS, dtype=jnp.int32), GRID_W)
    col = jnp.tile(jnp.arange(GRID_W, dtype=jnp.int32), ROWS)

    for layer in range(DEPTH):
        shift, scale, gate = adaln_params(c, w_mod[layer], b_mod[layer])
        shift_c, scale_c, gate_c = adaln_params(c_ctx, w_mod[layer], b_mod[layer])

        h_ctx = rms_norm(ctx, norm_g[layer]) * (1.0 + scale_c) + shift_c
        qc, kc, vc, ga_c, b_c, cg_c, hh_c, gc_c = split_proj(h_ctx @ w_in[layer])
        qc, kc, vc = qk_heads(qc, kc, vc, q_norm_g[layer], k_norm_g[layer])

        h = rms_norm(x, norm_g[layer]) * (1.0 + scale[:, None, :]) + shift[:, None, :]
        q, k, v, ga, b, cg, hh, gc = split_proj(h @ w_in[layer])
        q, k, v = qk_heads(q, k, v, q_norm_g[layer], k_norm_g[layer])
        q = axial_rope(q, row, col)
        k = axial_rope(k, row, col)
        attn = jax.nn.silu(ga) * latent_attention(q, k, v, kc, vc)
        conv = conv_branch(b, cg, hh, gc, conv_w[layer])
        y = jnp.concatenate([attn, conv], axis=-1) @ w_out[layer]
        x = x + gate[:, None, :] * y

        if layer < DEPTH - 1:
            attn_c = jax.nn.silu(ga_c) * context_attention(qc, kc, vc)
            conv_c = conv_branch(b_c, cg_c, hh_c, gc_c, conv_w[layer])
            y_c = jnp.concatenate([attn_c, conv_c], axis=-1) @ w_out[layer]
            ctx = ctx + gate_c * y_c

    return rms_norm(x, final_norm_g)
```

```python
import functools
import math

import jax
import jax.numpy as jnp
from jax import lax
from jax.experimental import pallas as pl
from jax.experimental.pallas import tpu as pltpu

D_MODEL = 2048
CTX_LEN = 256
GRID_W = 64
ATTN_WIDTH = 1024
CONV_WIDTH = 1024
HEAD_DIM = 128
N_HEADS = 8
N_KV_HEADS = 2
GROUP = N_HEADS // N_KV_HEADS
KV_WIDTH = N_KV_HEADS * HEAD_DIM
ROPE_THETA = 10000.0
EPS = 1e-6

LANES = 128
ROW_TILE = 512
COL_TILE = 512
KEY_CHUNK = 512
Q_TILE = 512
HALO_ROWS = 16
MOD_ROWS = 8
MOD_COL_TILE = 768
VMEM_LIMIT = 56 * 1024 * 1024

_F32 = jnp.float32
_BF16 = jnp.bfloat16


def _silu(v):
    return v * (1.0 / (1.0 + jnp.exp(-v)))


def _params(n_axes):
    return pltpu.CompilerParams(dimension_semantics=("arbitrary",) * n_axes,
                                vmem_limit_bytes=VMEM_LIMIT)


def _adaln_kernel(cb_ref, w_ref, b_ref, o_ref, s_scr):
    @pl.when(pl.program_id(0) == 0)
    def _():
        s_scr[...] = _silu(cb_ref[...])

    o_ref[...] = jnp.zeros(o_ref.shape, _F32)
    for c in range(o_ref.shape[1] // LANES):
        cols = slice(c * LANES, (c + 1) * LANES)
        wc = w_ref[:, cols]
        for r in range(3):
            acc = jnp.sum(wc * s_scr[r], axis=0, keepdims=True)
            o_ref[r:r + 1, cols] = acc + b_ref[:, cols]


def _adaln(cond_b, w_mod, b_mod):
    d, n = w_mod.shape
    return pl.pallas_call(
        _adaln_kernel,
        grid=(n // MOD_COL_TILE,),
        in_specs=[pl.BlockSpec((3, d, LANES), lambda j: (0, 0, 0)),
                  pl.BlockSpec((d, MOD_COL_TILE), lambda j: (0, j)),
                  pl.BlockSpec((1, MOD_COL_TILE), lambda j: (0, j))],
        out_specs=pl.BlockSpec((MOD_ROWS, MOD_COL_TILE), lambda j: (0, j)),
        out_shape=jax.ShapeDtypeStruct((MOD_ROWS, n), _F32),
        scratch_shapes=[pltpu.VMEM((3, d, LANES), _F32)],
        compiler_params=_params(1),
        name="adaln",
    )(cond_b, w_mod, b_mod)


def _prenorm_kernel(x_ref, g_ref, scale_ref, shift_ref, h_ref):
    xf = x_ref[...]
    y = xf * lax.rsqrt(jnp.mean(xf * xf, axis=-1, keepdims=True) + EPS) * g_ref[...]
    h_ref[...] = (y * (1.0 + scale_ref[0]) + shift_ref[0]).astype(_BF16)


def _prenorm(x2d, g, scale, shift, tm, tiles_per_mod):
    m, d = x2d.shape
    mod_spec = pl.BlockSpec((1, 1, d), lambda i: (i // tiles_per_mod, 0, 0))
    return pl.pallas_call(
        _prenorm_kernel,
        grid=(m // tm,),
        in_specs=[pl.BlockSpec((tm, d), lambda i: (i, 0)),
                  pl.BlockSpec((1, d), lambda i: (0, 0)),
                  mod_spec, mod_spec],
        out_specs=pl.BlockSpec((tm, d), lambda i: (i, 0)),
        out_shape=jax.ShapeDtypeStruct((m, d), _BF16),
        compiler_params=_params(1),
        name="prenorm",
    )(x2d, g, scale, shift)


def _norm_rope(xh, g, cos, sin_lo, sin_hi):
    y = xh * lax.rsqrt(jnp.mean(xh * xh, axis=-1, keepdims=True) + EPS) * g
    return (y * cos + pltpu.roll(y, 3 * HEAD_DIM // 4, axis=1) * sin_lo
            + pltpu.roll(y, HEAD_DIM // 4, axis=1) * sin_hi)


def _proj_q_kernel(h_ref, w_ref, g_ref, cos_ref, slo_ref, shi_ref, q_ref):
    acc = jnp.dot(h_ref[...], w_ref[...], preferred_element_type=_F32)
    cos, slo, shi, g = cos_ref[...], slo_ref[...], shi_ref[...], g_ref[...]
    for hh in range(q_ref.shape[1]):
        xh = acc[:, hh * HEAD_DIM:(hh + 1) * HEAD_DIM]
        q_ref[0, hh] = _norm_rope(xh, g, cos, slo, shi).astype(_BF16)


def _proj_kv_kernel(h_ref, w_ref, g_ref, cos_ref, slo_ref, shi_ref, k_ref, vt_ref):
    acc = jnp.dot(h_ref[...], w_ref[...], preferred_element_type=_F32)
    cos, slo, shi, g = cos_ref[...], slo_ref[...], shi_ref[...], g_ref[...]
    for hh in range(N_KV_HEADS):
        xh = acc[:, hh * HEAD_DIM:(hh + 1) * HEAD_DIM]
        k_ref[0, hh] = _norm_rope(xh, g, cos, slo, shi).astype(_BF16)
        vh = acc[:, KV_WIDTH + hh * HEAD_DIM:KV_WIDTH + (hh + 1) * HEAD_DIM]
        vt_ref[0, hh, 0] = vh.T.astype(_BF16)


def _proj_gate_kernel(h_ref, w_ref, o_ref):
    acc = jnp.dot(h_ref[...], w_ref[...], preferred_element_type=_F32)
    o_ref[...] = _silu(acc).astype(_BF16)


def _proj_conv_kernel(h_ref, wb_ref, wcg_ref, wh_ref, wgc_ref, u_ref, e_ref):
    h = h_ref[...]
    cg = jnp.dot(h, wcg_ref[...], preferred_element_type=_F32)
    hh = jnp.dot(h, wh_ref[...], preferred_element_type=_F32)
    u_ref[...] = (cg * hh).astype(_BF16)
    b = jnp.dot(h, wb_ref[...], preferred_element_type=_F32)
    gc = jnp.dot(h, wgc_ref[...], preferred_element_type=_F32)
    e_ref[...] = (b * _silu(gc)).astype(_BF16)


def _rope_specs(tm, seq):
    tiles = seq // tm
    return [pl.BlockSpec((tm, HEAD_DIM), lambda i, *_: (i % tiles, 0))] * 3


def _proj_q(h, w_bf, g, tables, batch, seq, tm):
    m, d = h.shape
    heads_per_step = COL_TILE // HEAD_DIM
    tiles = seq // tm
    return pl.pallas_call(
        _proj_q_kernel,
        grid=(m // tm, ATTN_WIDTH // COL_TILE),
        in_specs=[pl.BlockSpec((tm, d), lambda i, j: (i, 0)),
                  pl.BlockSpec((d, COL_TILE), lambda i, j: (0, j)),
                  pl.BlockSpec((1, HEAD_DIM), lambda i, j: (0, 0))] + _rope_specs(tm, seq),
        out_specs=pl.BlockSpec((1, heads_per_step, tm, HEAD_DIM),
                               lambda i, j: (i // tiles, j, i % tiles, 0)),
        out_shape=jax.ShapeDtypeStruct((batch, N_HEADS, seq, HEAD_DIM), _BF16),
        compiler_params=_params(2),
        name="proj_q",
    )(h, w_bf, g, *tables)


def _proj_kv(h, w_bf, g, tables, batch, seq, tm):
    m, d = h.shape
    tiles = seq // tm
    kv_col_block = ATTN_WIDTH // COL_TILE
    return pl.pallas_call(
        _proj_kv_kernel,
        grid=(m // tm,),
        in_specs=[pl.BlockSpec((tm, d), lambda i: (i, 0)),
                  pl.BlockSpec((d, 2 * KV_WIDTH), lambda i: (0, kv_col_block)),
                  pl.BlockSpec((1, HEAD_DIM), lambda i: (0, 0))] + _rope_specs(tm, seq),
        out_specs=[pl.BlockSpec((1, N_KV_HEADS, tm, HEAD_DIM), lambda i: (i // tiles, 0, i % tiles, 0)),
                   pl.BlockSpec((1, N_KV_HEADS, 1, HEAD_DIM, tm),
                                lambda i: (i // tiles, 0, i % tiles, 0, 0))],
        out_shape=[jax.ShapeDtypeStruct((batch, N_KV_HEADS, seq, HEAD_DIM), _BF16),
                   jax.ShapeDtypeStruct((batch, N_KV_HEADS, tiles, HEAD_DIM, tm), _BF16)],
        compiler_params=_params(1),
        name="proj_kv",
    )(h, w_bf, g, *tables)


def _proj_gate(h, w_bf, tm):
    m, d = h.shape
    first = (ATTN_WIDTH + 2 * KV_WIDTH) // COL_TILE
    return pl.pallas_call(
        _proj_gate_kernel,
        grid=(m // tm, ATTN_WIDTH // COL_TILE),
        in_specs=[pl.BlockSpec((tm, d), lambda i, j: (i, 0)),
                  pl.BlockSpec((d, COL_TILE), lambda i, j: (0, first + j))],
        out_specs=pl.BlockSpec((tm, COL_TILE), lambda i, j: (i, j)),
        out_shape=jax.ShapeDtypeStruct((m, ATTN_WIDTH), _BF16),
        compiler_params=_params(2),
        name="proj_gate",
    )(h, w_bf)


def _proj_conv(h, w_bf, tm):
    m, d = h.shape
    first = (2 * ATTN_WIDTH + 2 * KV_WIDTH) // COL_TILE
    step = CONV_WIDTH // COL_TILE

    def w_spec(group):
        return pl.BlockSpec((d, COL_TILE), lambda i, j: (0, first + group * step + j))

    out_spec = pl.BlockSpec((tm, COL_TILE), lambda i, j: (i, j))
    return pl.pallas_call(
        _proj_conv_kernel,
        grid=(m // tm, step),
        in_specs=[pl.BlockSpec((tm, d), lambda i, j: (i, 0)),
                  w_spec(0), w_spec(1), w_spec(2), w_spec(3)],
        out_specs=[out_spec, out_spec],
        out_shape=[jax.ShapeDtypeStruct((m, CONV_WIDTH), _BF16)] * 2,
        compiler_params=_params(2),
        name="proj_conv",
    )(h, w_bf, w_bf, w_bf, w_bf)


def _attn_kernel(q_ref, k_ref, vt_ref, kc_ref, vtc_ref, ga_ref, o_ref,
                 s_scr, sc_scr, m_scr, l_scr, acc_scr):
    n_chunks, kc = k_ref.shape[2], k_ref.shape[3]
    tq = q_ref.shape[2]
    nt = (((1,), (1,)), ((), ()))

    def fold(v):
        return v.reshape(v.shape[0] // 8, 8, v.shape[1])

    for g in range(GROUP):
        qh = q_ref[0, g]
        sc = lax.dot_general(kc_ref[0, 0], qh, nt, preferred_element_type=_F32)
        sc_scr[...] = sc
        m_scr[...] = fold(sc).max(axis=0)

        def scores(c, carry):
            s = lax.dot_general(k_ref[0, 0, c], qh, nt, preferred_element_type=_F32)
            s_scr[c] = s
            m_scr[...] = jnp.maximum(m_scr[...], fold(s).max(axis=0))
            return carry

        lax.fori_loop(0, n_chunks, scores, 0)
        m = m_scr[...].max(axis=0, keepdims=True)

        pc = jnp.exp(sc_scr[...] - m)
        l_scr[...] = fold(pc).sum(axis=0)
        acc_scr[...] = jnp.dot(vtc_ref[0, 0, 0], pc.astype(_BF16), preferred_element_type=_F32)

        def values(c, carry):
            p = jnp.exp(s_scr[c] - m)
            l_scr[...] += fold(p).sum(axis=0)
            acc_scr[...] += jnp.dot(vt_ref[0, 0, c], p.astype(_BF16), preferred_element_type=_F32)
            return carry

        lax.fori_loop(0, n_chunks, values, 0)
        inv_l = 1.0 / l_scr[...].sum(axis=0, keepdims=True)
        o = (acc_scr[...] * inv_l).T
        cols = slice(g * HEAD_DIM, (g + 1) * HEAD_DIM)
        o_ref[:, cols] = (o * ga_ref[:, cols].astype(_F32)).astype(_BF16)


def _attention(q, k, vt, k_ctx, vt_ctx, ga, batch, seq):
    n_chunks = seq // KEY_CHUNK
    ctx = k_ctx.shape[2]
    k5 = k.reshape(batch, N_KV_HEADS, n_chunks, KEY_CHUNK, HEAD_DIM)
    q_tiles = seq // Q_TILE
    group_w = GROUP * HEAD_DIM
    return pl.pallas_call(
        _attn_kernel,
        grid=(batch, N_KV_HEADS, q_tiles),
        in_specs=[pl.BlockSpec((1, GROUP, Q_TILE, HEAD_DIM), lambda b, kv, t: (b, kv, t, 0)),
                  pl.BlockSpec((1, 1, n_chunks, KEY_CHUNK, HEAD_DIM), lambda b, kv, t: (b, kv, 0, 0, 0)),
                  pl.BlockSpec((1, 1, n_chunks, HEAD_DIM, KEY_CHUNK), lambda b, kv, t: (b, kv, 0, 0, 0)),
                  pl.BlockSpec((1, 1, ctx, HEAD_DIM), lambda b, kv, t: (b, kv, 0, 0)),
                  pl.BlockSpec((1, 1, 1, HEAD_DIM, ctx), lambda b, kv, t: (b, kv, 0, 0, 0)),
                  pl.BlockSpec((Q_TILE, group_w), lambda b, kv, t: (b * q_tiles + t, kv))],
        out_specs=pl.BlockSpec((Q_TILE, group_w), lambda b, kv, t: (b * q_tiles + t, kv)),
        out_shape=jax.ShapeDtypeStruct((batch * seq, ATTN_WIDTH), _BF16),
        scratch_shapes=[pltpu.VMEM((n_chunks, KEY_CHUNK, Q_TILE), _F32),
                        pltpu.VMEM((ctx, Q_TILE), _F32),
                        pltpu.VMEM((8, Q_TILE), _F32),
                        pltpu.VMEM((8, Q_TILE), _F32),
                        pltpu.VMEM((HEAD_DIM, Q_TILE), _F32)],
        compiler_params=_params(3),
        name="attention",
    )(q, k5, vt, k_ctx, vt_ctx, ga)


def _out_kernel(tiles_per_seq, attn_ref, u_ref, e_ref, up_ref, un_ref, cw_ref, wa_ref, wc_ref,
                x_ref, gate_ref, gf_ref, o_ref):
    i = pl.program_id(0)
    tm = u_ref.shape[0]
    u = u_ref[...].astype(_F32)
    first = (i % tiles_per_seq) == 0
    last = (i % tiles_per_seq) == tiles_per_seq - 1
    prev_row = jnp.where(first, 0.0, up_ref[HALO_ROWS - 1:HALO_ROWS, :].astype(_F32))
    next_row = jnp.where(last, 0.0, un_ref[0:1, :].astype(_F32))
    rows = lax.broadcasted_iota(jnp.int32, (tm, 1), 0)
    u_before = jnp.where(rows == 0, prev_row, pltpu.roll(u, 1, axis=0))
    u_after = jnp.where(rows == tm - 1, next_row, pltpu.roll(u, tm - 1, axis=0))
    conv = u_before * cw_ref[0:1, :] + u * cw_ref[1:2, :] + u_after * cw_ref[2:3, :]
    cv = (e_ref[...].astype(_F32) * conv).astype(_BF16)
    y = jnp.dot(attn_ref[...], wa_ref[...], preferred_element_type=_F32)
    y += jnp.dot(cv, wc_ref[...], preferred_element_type=_F32)
    xn = x_ref[...] + gate_ref[0] * y
    o_ref[...] = xn * lax.rsqrt(jnp.mean(xn * xn, axis=-1, keepdims=True) + EPS) * gf_ref[...]


def _out_proj(attn, u, e, conv_w, w_out_bf, x2d, gate, gf, seq, tm):
    m, d = x2d.shape
    tiles_per_seq = seq // tm
    halo_per_tile = tm // HALO_ROWS
    n_halo = m // HALO_ROWS
    mix_spec = pl.BlockSpec((tm, ATTN_WIDTH), lambda i: (i, 0))
    return pl.pallas_call(
        functools.partial(_out_kernel, tiles_per_seq),
        grid=(m // tm,),
        in_specs=[mix_spec, mix_spec, mix_spec,
                  pl.BlockSpec((HALO_ROWS, CONV_WIDTH),
                               lambda i: (jnp.maximum(i * halo_per_tile - 1, 0), 0)),
                  pl.BlockSpec((HALO_ROWS, CONV_WIDTH),
                               lambda i: (jnp.minimum((i + 1) * halo_per_tile, n_halo - 1), 0)),
                  pl.BlockSpec((3, CONV_WIDTH), lambda i: (0, 0)),
                  pl.BlockSpec((ATTN_WIDTH, d), lambda i: (0, 0)),
                  pl.BlockSpec((CONV_WIDTH, d), lambda i: (1, 0)),
                  pl.BlockSpec((tm, d), lambda i: (i, 0)),
                  pl.BlockSpec((1, 1, d), lambda i: (i // tiles_per_seq, 0, 0)),
                  pl.BlockSpec((1, d), lambda i: (0, 0))],
        out_specs=pl.BlockSpec((tm, d), lambda i: (i, 0)),
        out_shape=jax.ShapeDtypeStruct((m, d), _F32),
        compiler_params=_params(1),
        name="out_proj",
    )(attn, u, e, u, u, conv_w, w_out_bf, w_out_bf, x2d, gate, gf)


def _rope_tables(seq):
    quarter = HEAD_DIM // 4
    t = jnp.arange(seq, dtype=jnp.int32)
    row = (t // GRID_W).astype(_F32)
    col = (t % GRID_W).astype(_F32)
    inv = ROPE_THETA ** (-jnp.arange(quarter, dtype=_F32) / quarter)
    ang = jnp.concatenate([row[:, None] * inv, row[:, None] * inv,
                           col[:, None] * inv, col[:, None] * inv], axis=1)
    lane = jnp.arange(HEAD_DIM) % (2 * quarter)
    sin = jnp.sin(ang)
    return (jnp.cos(ang), jnp.where(lane < quarter, -sin, 0.0), jnp.where(lane >= quarter, sin, 0.0))


def kernel(x, c, ctx, c_ctx, w_mod, b_mod, norm_g, w_in, q_norm_g, k_norm_g, conv_w, w_out,
           final_norm_g):
    batch, seq, d = x.shape
    ctx_len = ctx.shape[1]
    assert w_mod.shape[0] == 1, "single-layer kernel"
    assert seq % ROW_TILE == 0 and seq % KEY_CHUNK == 0 and seq % Q_TILE == 0

    cond = jnp.concatenate([c, c_ctx[None, :]], axis=0)
    cond_b = jnp.broadcast_to(cond[:, :, None], (batch + 1, d, LANES))
    mod = _adaln(cond_b, w_mod[0], b_mod[0][None, :])
    shift = mod[:batch + 1, None, 0 * d:1 * d]
    scale = mod[:batch + 1, None, 1 * d:2 * d]
    gate = mod[:batch, None, 2 * d:3 * d]

    w_in_bf = w_in[0].astype(_BF16)
    w_out_bf = w_out[0].astype(_BF16)
    g_in = norm_g[0][None, :]
    q_gain = (q_norm_g[0] * (1.0 / math.sqrt(HEAD_DIM)))[None, :]
    k_gain = k_norm_g[0][None, :]

    ctx2d = ctx.reshape(batch * ctx_len, d)
    h_ctx = _prenorm(ctx2d, g_in, scale[batch:], shift[batch:], ctx_len, batch)
    no_rope = (jnp.ones((ctx_len, HEAD_DIM), _F32), jnp.zeros((ctx_len, HEAD_DIM), _F32),
               jnp.zeros((ctx_len, HEAD_DIM), _F32))
    k_ctx, vt_ctx = _proj_kv(h_ctx, w_in_bf, k_gain, no_rope, batch, ctx_len, ctx_len)

    x2d = x.reshape(batch * seq, d)
    h = _prenorm(x2d, g_in, scale[:batch], shift[:batch], ROW_TILE, seq // ROW_TILE)
    tables = _rope_tables(seq)
    q = _proj_q(h, w_in_bf, q_gain, tables, batch, seq, ROW_TILE)
    k, vt = _proj_kv(h, w_in_bf, k_gain, tables, batch, seq, KEY_CHUNK)
    ga = _proj_gate(h, w_in_bf, ROW_TILE)
    u, e = _proj_conv(h, w_in_bf, ROW_TILE)

    attn = _attention(q, k, vt, k_ctx, vt_ctx, ga, batch, seq)
    out = _out_proj(attn, u, e, conv_w[0], w_out_bf, x2d, gate, final_norm_g[None, :], seq, ROW_TILE)
    return out.reshape(batch, seq, d)
```

```python
import functools
import math

import jax
import jax.numpy as jnp
from jax import lax
from jax.experimental import pallas as pl
from jax.experimental.pallas import tpu as pltpu

D_MODEL = 2048
CTX_LEN = 256
GRID_W = 64
ATTN_WIDTH = 1024
CONV_WIDTH = 1024
HEAD_DIM = 128
N_HEADS = 8
N_KV_HEADS = 2
GROUP = N_HEADS // N_KV_HEADS
KV_WIDTH = N_KV_HEADS * HEAD_DIM
ROPE_THETA = 10000.0
EPS = 1e-6

LANES = 128
ROW_TILE = 512
COL_TILE = 512
KEY_CHUNK = 512
Q_TILE = 512
HALO_ROWS = 16
MOD_ROWS = 8
MOD_COL_TILE = 768
VMEM_LIMIT = 56 * 1024 * 1024

_F32 = jnp.float32
_BF16 = jnp.bfloat16


def _silu(v):
    return v * (1.0 / (1.0 + jnp.exp(-v)))


def _params(n_axes):
    return pltpu.CompilerParams(dimension_semantics=("arbitrary",) * n_axes,
                                vmem_limit_bytes=VMEM_LIMIT)


def _adaln_kernel(cb_ref, w_ref, b_ref, o_ref, s_scr):
    @pl.when(pl.program_id(0) == 0)
    def _():
        s_scr[...] = _silu(cb_ref[...])

    o_ref[...] = jnp.zeros(o_ref.shape, _F32)
    for c in range(o_ref.shape[1] // LANES):
        cols = slice(c * LANES, (c + 1) * LANES)
        wc = w_ref[:, cols]
        for r in range(3):
            acc = jnp.sum(wc * s_scr[r], axis=0, keepdims=True)
            o_ref[r:r + 1, cols] = acc + b_ref[:, cols]


def _adaln(cond_b, w_mod, b_mod):
    d, n = w_mod.shape
    return pl.pallas_call(
        _adaln_kernel,
        grid=(n // MOD_COL_TILE,),
        in_specs=[pl.BlockSpec((3, d, LANES), lambda j: (0, 0, 0)),
                  pl.BlockSpec((d, MOD_COL_TILE), lambda j: (0, j)),
                  pl.BlockSpec((1, MOD_COL_TILE), lambda j: (0, j))],
        out_specs=pl.BlockSpec((MOD_ROWS, MOD_COL_TILE), lambda j: (0, j)),
        out_shape=jax.ShapeDtypeStruct((MOD_ROWS, n), _F32),
        scratch_shapes=[pltpu.VMEM((3, d, LANES), _F32)],
        compiler_params=_params(1),
        name="adaln",
    )(cond_b, w_mod, b_mod)


def _prenorm_kernel(x_ref, g_ref, scale_ref, shift_ref, h_ref):
    xf = x_ref[...]
    y = xf * lax.rsqrt(jnp.mean(xf * xf, axis=-1, keepdims=True) + EPS) * g_ref[...]
    h_ref[...] = (y * (1.0 + scale_ref[0]) + shift_ref[0]).astype(_BF16)


def _prenorm(x2d, g, scale, shift, tm, tiles_per_mod):
    m, d = x2d.shape
    mod_spec = pl.BlockSpec((1, 1, d), lambda i: (i // tiles_per_mod, 0, 0))
    return pl.pallas_call(
        _prenorm_kernel,
        grid=(m // tm,),
        in_specs=[pl.BlockSpec((tm, d), lambda i: (i, 0)),
                  pl.BlockSpec((1, d), lambda i: (0, 0)),
                  mod_spec, mod_spec],
        out_specs=pl.BlockSpec((tm, d), lambda i: (i, 0)),
        out_shape=jax.ShapeDtypeStruct((m, d), _BF16),
        compiler_params=_params(1),
        name="prenorm",
    )(x2d, g, scale, shift)


def _norm_rope(xh, g, cos, sin_lo, sin_hi):
    y = xh * lax.rsqrt(jnp.mean(xh * xh, axis=-1, keepdims=True) + EPS) * g
    return (y * cos + pltpu.roll(y, 3 * HEAD_DIM // 4, axis=1) * sin_lo
            + pltpu.roll(y, HEAD_DIM // 4, axis=1) * sin_hi)


def _proj_q_kernel(h_ref, w_ref, g_ref, cos_ref, slo_ref, shi_ref, q_ref):
    acc = jnp.dot(h_ref[...], w_ref[...], preferred_element_type=_F32)
    cos, slo, shi, g = cos_ref[...], slo_ref[...], shi_ref[...], g_ref[...]
    for hh in range(q_ref.shape[1]):
        xh = acc[:, hh * HEAD_DIM:(hh + 1) * HEAD_DIM]
        q_ref[0, hh] = _norm_rope(xh, g, cos, slo, shi).astype(_BF16)


def _proj_kv_kernel(h_ref, w_ref, g_ref, cos_ref, slo_ref, shi_ref, k_ref, vt_ref):
    acc = jnp.dot(h_ref[...], w_ref[...], preferred_element_type=_F32)
    cos, slo, shi, g = cos_ref[...], slo_ref[...], shi_ref[...], g_ref[...]
    for hh in range(N_KV_HEADS):
        xh = acc[:, hh * HEAD_DIM:(hh + 1) * HEAD_DIM]
        k_ref[0, hh] = _norm_rope(xh, g, cos, slo, shi).astype(_BF16)
        vh = acc[:, KV_WIDTH + hh * HEAD_DIM:KV_WIDTH + (hh + 1) * HEAD_DIM]
        vt_ref[0, hh, 0] = vh.T.astype(_BF16)


def _proj_gate_kernel(h_ref, w_ref, o_ref):
    acc = jnp.dot(h_ref[...], w_ref[...], preferred_element_type=_F32)
    o_ref[...] = _silu(acc).astype(_BF16)


def _proj_conv_kernel(h_ref, wb_ref, wcg_ref, wh_ref, wgc_ref, u_ref, e_ref):
    h = h_ref[...]
    cg = jnp.dot(h, wcg_ref[...], preferred_element_type=_F32)
    hh = jnp.dot(h, wh_ref[...], preferred_element_type=_F32)
    u_ref[...] = (cg * hh).astype(_BF16)
    b = jnp.dot(h, wb_ref[...], preferred_element_type=_F32)
    gc = jnp.dot(h, wgc_ref[...], preferred_element_type=_F32)
    e_ref[...] = (b * _silu(gc)).astype(_BF16)


def _rope_specs(tm, seq):
    tiles = seq // tm
    return [pl.BlockSpec((tm, HEAD_DIM), lambda i, *_: (i % tiles, 0))] * 3


def _proj_q(h, w_bf, g, tables, batch, seq, tm):
    m, d = h.shape
    heads_per_step = COL_TILE // HEAD_DIM
    tiles = seq // tm
    return pl.pallas_call(
        _proj_q_kernel,
        grid=(m // tm, ATTN_WIDTH // COL_TILE),
        in_specs=[pl.BlockSpec((tm, d), lambda i, j: (i, 0)),
                  pl.BlockSpec((d, COL_TILE), lambda i, j: (0, j)),
                  pl.BlockSpec((1, HEAD_DIM), lambda i, j: (0, 0))] + _rope_specs(tm, seq),
        out_specs=pl.BlockSpec((1, heads_per_step, tm, HEAD_DIM),
                               lambda i, j: (i // tiles, j, i % tiles, 0)),
        out_shape=jax.ShapeDtypeStruct((batch, N_HEADS, seq, HEAD_DIM), _BF16),
        compiler_params=_params(2),
        name="proj_q",
    )(h, w_bf, g, *tables)


def _proj_kv(h, w_bf, g, tables, batch, seq, tm):
    m, d = h.shape
    tiles = seq // tm
    kv_col_block = ATTN_WIDTH // COL_TILE
    return pl.pallas_call(
        _proj_kv_kernel,
        grid=(m // tm,),
        in_specs=[pl.BlockSpec((tm, d), lambda i: (i, 0)),
                  pl.BlockSpec((d, 2 * KV_WIDTH), lambda i: (0, kv_col_block)),
                  pl.BlockSpec((1, HEAD_DIM), lambda i: (0, 0))] + _rope_specs(tm, seq),
        out_specs=[pl.BlockSpec((1, N_KV_HEADS, tm, HEAD_DIM), lambda i: (i // tiles, 0, i % tiles, 0)),
                   pl.BlockSpec((1, N_KV_HEADS, 1, HEAD_DIM, tm),
                                lambda i: (i // tiles, 0, i % tiles, 0, 0))],
        out_shape=[jax.ShapeDtypeStruct((batch, N_KV_HEADS, seq, HEAD_DIM), _BF16),
                   jax.ShapeDtypeStruct((batch, N_KV_HEADS, tiles, HEAD_DIM, tm), _BF16)],
        compiler_params=_params(1),
        name="proj_kv",
    )(h, w_bf, g, *tables)


def _proj_gate(h, w_bf, tm):
    m, d = h.shape
    first = (ATTN_WIDTH + 2 * KV_WIDTH) // COL_TILE
    return pl.pallas_call(
        _proj_gate_kernel,
        grid=(m // tm, ATTN_WIDTH // COL_TILE),
        in_specs=[pl.BlockSpec((tm, d), lambda i, j: (i, 0)),
                  pl.BlockSpec((d, COL_TILE), lambda i, j: (0, first + j))],
        out_specs=pl.BlockSpec((tm, COL_TILE), lambda i, j: (i, j)),
        out_shape=jax.ShapeDtypeStruct((m, ATTN_WIDTH), _BF16),
        compiler_params=_params(2),
        name="proj_gate",
    )(h, w_bf)


def _proj_conv(h, w_bf, tm):
    m, d = h.shape
    first = (2 * ATTN_WIDTH + 2 * KV_WIDTH) // COL_TILE
    step = CONV_WIDTH // COL_TILE

    def w_spec(group):
        return pl.BlockSpec((d, COL_TILE), lambda i, j: (0, first + group * step + j))

    out_spec = pl.BlockSpec((tm, COL_TILE), lambda i, j: (i, j))
    return pl.pallas_call(
        _proj_conv_kernel,
        grid=(m // tm, step),
        in_specs=[pl.BlockSpec((tm, d), lambda i, j: (i, 0)),
                  w_spec(0), w_spec(1), w_spec(2), w_spec(3)],
        out_specs=[out_spec, out_spec],
        out_shape=[jax.ShapeDtypeStruct((m, CONV_WIDTH), _BF16)] * 2,
        compiler_params=_params(2),
        name="proj_conv",
    )(h, w_bf, w_bf, w_bf, w_bf)


def _attn_kernel(q_ref, k_ref, vt_ref, kc_ref, vtc_ref, ga_ref, o_ref,
                 s_even, s_odd, sc_even, sc_odd, m_even, m_odd, l_scr, acc_scr):
    n_chunks = k_ref.shape[2]
    nt = (((1,), (1,)), ((), ()))
    s_bufs, sc_bufs, m_bufs = (s_even, s_odd), (sc_even, sc_odd), (m_even, m_odd)

    def fold(v):
        return v.reshape(v.shape[0] // 8, 8, v.shape[1])

    for ph in range(GROUP + 1):
        score_head = ph if ph < GROUP else None
        value_head = ph - 1 if ph >= 1 else None

        if score_head is not None:
            s_w, sc_w, m_w = s_bufs[ph % 2], sc_bufs[ph % 2], m_bufs[ph % 2]
            qh = q_ref[0, score_head]
            sc = lax.dot_general(kc_ref[0, 0], qh, nt, preferred_element_type=_F32)
            sc_w[...] = sc
            m_w[...] = fold(sc).max(axis=0)
        if value_head is not None:
            s_r, sc_r, m_r = s_bufs[value_head % 2], sc_bufs[value_head % 2], m_bufs[value_head % 2]
            m = m_r[...].max(axis=0, keepdims=True)
            pc = jnp.exp2(sc_r[...] - m)
            l_scr[...] = fold(pc).sum(axis=0)
            acc_scr[...] = jnp.dot(vtc_ref[0, 0, 0], pc.astype(_BF16), preferred_element_type=_F32)

        def chunk(c, carry):
            if score_head is not None:
                s = lax.dot_general(k_ref[0, 0, c], qh, nt, preferred_element_type=_F32)
                s_w[c] = s
                m_w[...] = jnp.maximum(m_w[...], fold(s).max(axis=0))
            if value_head is not None:
                p = jnp.exp2(s_r[c] - m)
                l_scr[...] += fold(p).sum(axis=0)
                acc_scr[...] += jnp.dot(vt_ref[0, 0, c], p.astype(_BF16),
                                        preferred_element_type=_F32)
            return carry

        lax.fori_loop(0, n_chunks, chunk, 0, unroll=True)

        if value_head is not None:
            inv_l = 1.0 / l_scr[...].sum(axis=0, keepdims=True)
            o = (acc_scr[...] * inv_l).T
            cols = slice(value_head * HEAD_DIM, (value_head + 1) * HEAD_DIM)
            o_ref[:, cols] = (o * ga_ref[:, cols].astype(_F32)).astype(_BF16)


def _attention(q, k, vt, k_ctx, vt_ctx, ga, batch, seq):
    n_chunks = seq // KEY_CHUNK
    ctx = k_ctx.shape[2]
    k5 = k.reshape(batch, N_KV_HEADS, n_chunks, KEY_CHUNK, HEAD_DIM)
    q_tiles = seq // Q_TILE
    group_w = GROUP * HEAD_DIM
    score_buf = pltpu.VMEM((n_chunks, KEY_CHUNK, Q_TILE), _F32)
    ctx_score_buf = pltpu.VMEM((ctx, Q_TILE), _F32)
    fold_buf = pltpu.VMEM((8, Q_TILE), _F32)
    return pl.pallas_call(
        _attn_kernel,
        grid=(batch, N_KV_HEADS, q_tiles),
        in_specs=[pl.BlockSpec((1, GROUP, Q_TILE, HEAD_DIM), lambda b, kv, t: (b, kv, t, 0)),
                  pl.BlockSpec((1, 1, n_chunks, KEY_CHUNK, HEAD_DIM), lambda b, kv, t: (b, kv, 0, 0, 0)),
                  pl.BlockSpec((1, 1, n_chunks, HEAD_DIM, KEY_CHUNK), lambda b, kv, t: (b, kv, 0, 0, 0)),
                  pl.BlockSpec((1, 1, ctx, HEAD_DIM), lambda b, kv, t: (b, kv, 0, 0)),
                  pl.BlockSpec((1, 1, 1, HEAD_DIM, ctx), lambda b, kv, t: (b, kv, 0, 0, 0)),
                  pl.BlockSpec((Q_TILE, group_w), lambda b, kv, t: (b * q_tiles + t, kv))],
        out_specs=pl.BlockSpec((Q_TILE, group_w), lambda b, kv, t: (b * q_tiles + t, kv)),
        out_shape=jax.ShapeDtypeStruct((batch * seq, ATTN_WIDTH), _BF16),
        scratch_shapes=[score_buf, score_buf, ctx_score_buf, ctx_score_buf, fold_buf, fold_buf,
                        fold_buf, pltpu.VMEM((HEAD_DIM, Q_TILE), _F32)],
        compiler_params=_params(3),
        name="attention",
    )(q, k5, vt, k_ctx, vt_ctx, ga)


def _out_kernel(tiles_per_seq, attn_ref, u_ref, e_ref, up_ref, un_ref, cw_ref, wa_ref, wc_ref,
                x_ref, gate_ref, gf_ref, o_ref):
    i = pl.program_id(0)
    tm = u_ref.shape[0]
    u = u_ref[...].astype(_F32)
    first = (i % tiles_per_seq) == 0
    last = (i % tiles_per_seq) == tiles_per_seq - 1
    prev_row = jnp.where(first, 0.0, up_ref[HALO_ROWS - 1:HALO_ROWS, :].astype(_F32))
    next_row = jnp.where(last, 0.0, un_ref[0:1, :].astype(_F32))
    rows = lax.broadcasted_iota(jnp.int32, (tm, 1), 0)
    u_before = jnp.where(rows == 0, prev_row, pltpu.roll(u, 1, axis=0))
    u_after = jnp.where(rows == tm - 1, next_row, pltpu.roll(u, tm - 1, axis=0))
    conv = u_before * cw_ref[0:1, :] + u * cw_ref[1:2, :] + u_after * cw_ref[2:3, :]
    cv = (e_ref[...].astype(_F32) * conv).astype(_BF16)
    y = jnp.dot(attn_ref[...], wa_ref[...], preferred_element_type=_F32)
    y += jnp.dot(cv, wc_ref[...], preferred_element_type=_F32)
    xn = x_ref[...] + gate_ref[0] * y
    o_ref[...] = xn * lax.rsqrt(jnp.mean(xn * xn, axis=-1, keepdims=True) + EPS) * gf_ref[...]


def _out_proj(attn, u, e, conv_w, w_out_bf, x2d, gate, gf, seq, tm):
    m, d = x2d.shape
    tiles_per_seq = seq // tm
    halo_per_tile = tm // HALO_ROWS
    n_halo = m // HALO_ROWS
    mix_spec = pl.BlockSpec((tm, ATTN_WIDTH), lambda i: (i, 0))
    return pl.pallas_call(
        functools.partial(_out_kernel, tiles_per_seq),
        grid=(m // tm,),
        in_specs=[mix_spec, mix_spec, mix_spec,
                  pl.BlockSpec((HALO_ROWS, CONV_WIDTH),
                               lambda i: (jnp.maximum(i * halo_per_tile - 1, 0), 0)),
                  pl.BlockSpec((HALO_ROWS, CONV_WIDTH),
                               lambda i: (jnp.minimum((i + 1) * halo_per_tile, n_halo - 1), 0)),
                  pl.BlockSpec((3, CONV_WIDTH), lambda i: (0, 0)),
                  pl.BlockSpec((ATTN_WIDTH, d), lambda i: (0, 0)),
                  pl.BlockSpec((CONV_WIDTH, d), lambda i: (1, 0)),
                  pl.BlockSpec((tm, d), lambda i: (i, 0)),
                  pl.BlockSpec((1, 1, d), lambda i: (i // tiles_per_seq, 0, 0)),
                  pl.BlockSpec((1, d), lambda i: (0, 0))],
        out_specs=pl.BlockSpec((tm, d), lambda i: (i, 0)),
        out_shape=jax.ShapeDtypeStruct((m, d), _F32),
        compiler_params=_params(1),
        name="out_proj",
    )(attn, u, e, u, u, conv_w, w_out_bf, w_out_bf, x2d, gate, gf)


def _rope_tables(seq):
    quarter = HEAD_DIM // 4
    t = jnp.arange(seq, dtype=jnp.int32)
    row = (t // GRID_W).astype(_F32)
    col = (t % GRID_W).astype(_F32)
    inv = ROPE_THETA ** (-jnp.arange(quarter, dtype=_F32) / quarter)
    ang = jnp.concatenate([row[:, None] * inv, row[:, None] * inv,
                           col[:, None] * inv, col[:, None] * inv], axis=1)
    lane = jnp.arange(HEAD_DIM) % (2 * quarter)
    sin = jnp.sin(ang)
    return (jnp.cos(ang), jnp.where(lane < quarter, -sin, 0.0), jnp.where(lane >= quarter, sin, 0.0))


def kernel(x, c, ctx, c_ctx, w_mod, b_mod, norm_g, w_in, q_norm_g, k_norm_g, conv_w, w_out,
           final_norm_g):
    batch, seq, d = x.shape
    ctx_len = ctx.shape[1]
    assert w_mod.shape[0] == 1, "single-layer kernel"
    assert seq % ROW_TILE == 0 and seq % KEY_CHUNK == 0 and seq % Q_TILE == 0

    cond = jnp.concatenate([c, c_ctx[None, :]], axis=0)
    cond_b = jnp.broadcast_to(cond[:, :, None], (batch + 1, d, LANES))
    mod = _adaln(cond_b, w_mod[0], b_mod[0][None, :])
    shift = mod[:batch + 1, None, 0 * d:1 * d]
    scale = mod[:batch + 1, None, 1 * d:2 * d]
    gate = mod[:batch, None, 2 * d:3 * d]

    w_in_bf = w_in[0].astype(_BF16)
    w_out_bf = w_out[0].astype(_BF16)
    g_in = norm_g[0][None, :]
    q_gain = (q_norm_g[0] * (math.log2(math.e) / math.sqrt(HEAD_DIM)))[None, :]
    k_gain = k_norm_g[0][None, :]

    ctx2d = ctx.reshape(batch * ctx_len, d)
    h_ctx = _prenorm(ctx2d, g_in, scale[batch:], shift[batch:], ctx_len, batch)
    no_rope = (jnp.ones((ctx_len, HEAD_DIM), _F32), jnp.zeros((ctx_len, HEAD_DIM), _F32),
               jnp.zeros((ctx_len, HEAD_DIM), _F32))
    k_ctx, vt_ctx = _proj_kv(h_ctx, w_in_bf, k_gain, no_rope, batch, ctx_len, ctx_len)

    x2d = x.reshape(batch * seq, d)
    h = _prenorm(x2d, g_in, scale[:batch], shift[:batch], ROW_TILE, seq // ROW_TILE)
    tables = _rope_tables(seq)
    q = _proj_q(h, w_in_bf, q_gain, tables, batch, seq, ROW_TILE)
    k, vt = _proj_kv(h, w_in_bf, k_gain, tables, batch, seq, KEY_CHUNK)
    ga = _proj_gate(h, w_in_bf, ROW_TILE)
    u, e = _proj_conv(h, w_in_bf, ROW_TILE)

    attn = _attention(q, k, vt, k_ctx, vt_ctx, ga, batch, seq)
    out = _out_proj(attn, u, e, conv_w[0], w_out_bf, x2d, gate, final_norm_g[None, :], seq, ROW_TILE)
    return out.reshape(batch, seq, d)
```

```python
import functools
import math

import jax
import jax.numpy as jnp
from jax import lax
from jax.experimental import pallas as pl
from jax.experimental.pallas import tpu as pltpu

D_MODEL = 2048
CTX_LEN = 256
GRID_W = 64
ATTN_WIDTH = 1024
CONV_WIDTH = 1024
HEAD_DIM = 128
N_HEADS = 8
N_KV_HEADS = 2
GROUP = N_HEADS // N_KV_HEADS
KV_WIDTH = N_KV_HEADS * HEAD_DIM
ROPE_THETA = 10000.0
EPS = 1e-6

LANES = 128
ROW_TILE = 512
COL_TILE = 512
KEY_CHUNK = 512
Q_TILE = 512
HALO_ROWS = 16
MOD_ROWS = 8
MOD_COL_TILE = 768
VMEM_LIMIT = 56 * 1024 * 1024

_F32 = jnp.float32
_BF16 = jnp.bfloat16


def _silu(v):
    return v * (1.0 / (1.0 + jnp.exp(-v)))


def _params(n_axes):
    return pltpu.CompilerParams(dimension_semantics=("arbitrary",) * n_axes,
                                vmem_limit_bytes=VMEM_LIMIT)


def _adaln_kernel(cb_ref, w_ref, b_ref, o_ref, s_scr):
    @pl.when(pl.program_id(0) == 0)
    def _():
        s_scr[...] = _silu(cb_ref[...])

    o_ref[...] = jnp.zeros(o_ref.shape, _F32)
    for c in range(o_ref.shape[1] // LANES):
        cols = slice(c * LANES, (c + 1) * LANES)
        wc = w_ref[:, cols]
        for r in range(3):
            acc = jnp.sum(wc * s_scr[r], axis=0, keepdims=True)
            o_ref[r:r + 1, cols] = acc + b_ref[:, cols]


def _adaln(cond_b, w_mod, b_mod):
    d, n = w_mod.shape
    return pl.pallas_call(
        _adaln_kernel,
        grid=(n // MOD_COL_TILE,),
        in_specs=[pl.BlockSpec((3, d, LANES), lambda j: (0, 0, 0)),
                  pl.BlockSpec((d, MOD_COL_TILE), lambda j: (0, j)),
                  pl.BlockSpec((1, MOD_COL_TILE), lambda j: (0, j))],
        out_specs=pl.BlockSpec((MOD_ROWS, MOD_COL_TILE), lambda j: (0, j)),
        out_shape=jax.ShapeDtypeStruct((MOD_ROWS, n), _F32),
        scratch_shapes=[pltpu.VMEM((3, d, LANES), _F32)],
        compiler_params=_params(1),
        name="adaln",
    )(cond_b, w_mod, b_mod)


def _prenorm_kernel(x_ref, g_ref, scale_ref, shift_ref, h_ref):
    xf = x_ref[...]
    y = xf * lax.rsqrt(jnp.mean(xf * xf, axis=-1, keepdims=True) + EPS) * g_ref[...]
    h_ref[...] = (y * (1.0 + scale_ref[0]) + shift_ref[0]).astype(_BF16)


def _prenorm(x2d, g, scale, shift, tm, tiles_per_mod):
    m, d = x2d.shape
    mod_spec = pl.BlockSpec((1, 1, d), lambda i: (i // tiles_per_mod, 0, 0))
    return pl.pallas_call(
        _prenorm_kernel,
        grid=(m // tm,),
        in_specs=[pl.BlockSpec((tm, d), lambda i: (i, 0)),
                  pl.BlockSpec((1, d), lambda i: (0, 0)),
                  mod_spec, mod_spec],
        out_specs=pl.BlockSpec((tm, d), lambda i: (i, 0)),
        out_shape=jax.ShapeDtypeStruct((m, d), _BF16),
        compiler_params=_params(1),
        name="prenorm",
    )(x2d, g, scale, shift)


def _norm_rope(xh, g, cos, sin_lo, sin_hi):
    y = xh * lax.rsqrt(jnp.mean(xh * xh, axis=-1, keepdims=True) + EPS) * g
    return (y * cos + pltpu.roll(y, 3 * HEAD_DIM // 4, axis=1) * sin_lo
            + pltpu.roll(y, HEAD_DIM // 4, axis=1) * sin_hi)


def _store_kv(acc, g, cos, slo, shi, k_ref, vt_ref):
    for hh in range(N_KV_HEADS):
        xh = acc[:, hh * HEAD_DIM:(hh + 1) * HEAD_DIM]
        k_ref[0, hh] = _norm_rope(xh, g, cos, slo, shi).astype(_BF16)
        vh = acc[:, KV_WIDTH + hh * HEAD_DIM:KV_WIDTH + (hh + 1) * HEAD_DIM]
        vt_ref[0, hh, 0] = vh.T.astype(_BF16)


def _proj_kv_kernel(h_ref, w_ref, g_ref, cos_ref, slo_ref, shi_ref, k_ref, vt_ref):
    acc = jnp.dot(h_ref[...], w_ref[...], preferred_element_type=_F32)
    _store_kv(acc, g_ref[...], cos_ref[...], slo_ref[...], shi_ref[...], k_ref, vt_ref)


def _proj_attn_kernel(x_ref, gin_ref, scale_ref, shift_ref, w_ref, qg_ref, kg_ref,
                      cos_ref, slo_ref, shi_ref, h_ref, q_ref, k_ref, vt_ref, ga_ref):
    xf = x_ref[...]
    y = xf * lax.rsqrt(jnp.mean(xf * xf, axis=-1, keepdims=True) + EPS) * gin_ref[...]
    h = (y * (1.0 + scale_ref[0]) + shift_ref[0]).astype(_BF16)
    h_ref[...] = h
    cos, slo, shi = cos_ref[...], slo_ref[...], shi_ref[...]
    heads_per_dot = COL_TILE // HEAD_DIM
    for blk in range(ATTN_WIDTH // COL_TILE):
        acc = jnp.dot(h, w_ref[:, blk * COL_TILE:(blk + 1) * COL_TILE], preferred_element_type=_F32)
        for hh in range(heads_per_dot):
            xh = acc[:, hh * HEAD_DIM:(hh + 1) * HEAD_DIM]
            q_ref[0, blk * heads_per_dot + hh] = _norm_rope(xh, qg_ref[...], cos, slo, shi).astype(_BF16)
    acc = jnp.dot(h, w_ref[:, ATTN_WIDTH:ATTN_WIDTH + 2 * KV_WIDTH], preferred_element_type=_F32)
    _store_kv(acc, kg_ref[...], cos, slo, shi, k_ref, vt_ref)
    gate_col = ATTN_WIDTH + 2 * KV_WIDTH
    for blk in range(ATTN_WIDTH // COL_TILE):
        cols = slice(gate_col + blk * COL_TILE, gate_col + (blk + 1) * COL_TILE)
        acc = jnp.dot(h, w_ref[:, cols], preferred_element_type=_F32)
        ga_ref[:, blk * COL_TILE:(blk + 1) * COL_TILE] = _silu(acc).astype(_BF16)


def _proj_conv_kernel(h_ref, wb_ref, wcg_ref, wh_ref, wgc_ref, u_ref, e_ref):
    h = h_ref[...]
    cg = jnp.dot(h, wcg_ref[...], preferred_element_type=_F32)
    hh = jnp.dot(h, wh_ref[...], preferred_element_type=_F32)
    u_ref[...] = (cg * hh).astype(_BF16)
    b = jnp.dot(h, wb_ref[...], preferred_element_type=_F32)
    gc = jnp.dot(h, wgc_ref[...], preferred_element_type=_F32)
    e_ref[...] = (b * _silu(gc)).astype(_BF16)


def _rope_specs(tm, seq):
    tiles = seq // tm
    return [pl.BlockSpec((tm, HEAD_DIM), lambda i, *_: (i % tiles, 0))] * 3


def _proj_attn(x2d, gin, scale, shift, w_bf, q_gain, k_gain, tables, batch, seq, tm):
    m, d = x2d.shape
    tiles = seq // tm
    n_cols = 2 * ATTN_WIDTH + 2 * KV_WIDTH
    mod_spec = pl.BlockSpec((1, 1, d), lambda i: (i // tiles, 0, 0))
    gain_spec = pl.BlockSpec((1, HEAD_DIM), lambda i: (0, 0))
    return pl.pallas_call(
        _proj_attn_kernel,
        grid=(m // tm,),
        in_specs=[pl.BlockSpec((tm, d), lambda i: (i, 0)),
                  pl.BlockSpec((1, d), lambda i: (0, 0)),
                  mod_spec, mod_spec,
                  pl.BlockSpec((d, n_cols), lambda i: (0, 0)),
                  gain_spec, gain_spec] + _rope_specs(tm, seq),
        out_specs=[pl.BlockSpec((tm, d), lambda i: (i, 0)),
                   pl.BlockSpec((1, N_HEADS, tm, HEAD_DIM), lambda i: (i // tiles, 0, i % tiles, 0)),
                   pl.BlockSpec((1, N_KV_HEADS, tm, HEAD_DIM), lambda i: (i // tiles, 0, i % tiles, 0)),
                   pl.BlockSpec((1, N_KV_HEADS, 1, HEAD_DIM, tm),
                                lambda i: (i // tiles, 0, i % tiles, 0, 0)),
                   pl.BlockSpec((tm, ATTN_WIDTH), lambda i: (i, 0))],
        out_shape=[jax.ShapeDtypeStruct((m, d), _BF16),
                   jax.ShapeDtypeStruct((batch, N_HEADS, seq, HEAD_DIM), _BF16),
                   jax.ShapeDtypeStruct((batch, N_KV_HEADS, seq, HEAD_DIM), _BF16),
                   jax.ShapeDtypeStruct((batch, N_KV_HEADS, tiles, HEAD_DIM, tm), _BF16),
                   jax.ShapeDtypeStruct((m, ATTN_WIDTH), _BF16)],
        compiler_params=_params(1),
        name="proj_attn",
    )(x2d, gin, scale, shift, w_bf, q_gain, k_gain, *tables)


def _proj_kv(h, w_bf, g, tables, batch, seq, tm):
    m, d = h.shape
    tiles = seq // tm
    kv_col_block = ATTN_WIDTH // COL_TILE
    return pl.pallas_call(
        _proj_kv_kernel,
        grid=(m // tm,),
        in_specs=[pl.BlockSpec((tm, d), lambda i: (i, 0)),
                  pl.BlockSpec((d, 2 * KV_WIDTH), lambda i: (0, kv_col_block)),
                  pl.BlockSpec((1, HEAD_DIM), lambda i: (0, 0))] + _rope_specs(tm, seq),
        out_specs=[pl.BlockSpec((1, N_KV_HEADS, tm, HEAD_DIM), lambda i: (i // tiles, 0, i % tiles, 0)),
                   pl.BlockSpec((1, N_KV_HEADS, 1, HEAD_DIM, tm),
                                lambda i: (i // tiles, 0, i % tiles, 0, 0))],
        out_shape=[jax.ShapeDtypeStruct((batch, N_KV_HEADS, seq, HEAD_DIM), _BF16),
                   jax.ShapeDtypeStruct((batch, N_KV_HEADS, tiles, HEAD_DIM, tm), _BF16)],
        compiler_params=_params(1),
        name="proj_kv",
    )(h, w_bf, g, *tables)


def _proj_conv(h, w_bf, tm):
    m, d = h.shape
    first = (2 * ATTN_WIDTH + 2 * KV_WIDTH) // COL_TILE
    step = CONV_WIDTH // COL_TILE

    def w_spec(group):
        return pl.BlockSpec((d, COL_TILE), lambda i, j: (0, first + group * step + j))

    out_spec = pl.BlockSpec((tm, COL_TILE), lambda i, j: (i, j))
    return pl.pallas_call(
        _proj_conv_kernel,
        grid=(m // tm, step),
        in_specs=[pl.BlockSpec((tm, d), lambda i, j: (i, 0)),
                  w_spec(0), w_spec(1), w_spec(2), w_spec(3)],
        out_specs=[out_spec, out_spec],
        out_shape=[jax.ShapeDtypeStruct((m, CONV_WIDTH), _BF16)] * 2,
        compiler_params=_params(2),
        name="proj_conv",
    )(h, w_bf, w_bf, w_bf, w_bf)


def _attn_kernel(q_ref, k_ref, vt_ref, kc_ref, vtc_ref, ga_ref, o_ref,
                 s_even, s_odd, sc_even, sc_odd, m_even, m_odd, l_scr, acc_scr):
    n_chunks = k_ref.shape[2]
    nt = (((1,), (1,)), ((), ()))
    s_bufs, sc_bufs, m_bufs = (s_even, s_odd), (sc_even, sc_odd), (m_even, m_odd)

    def fold(v):
        return v.reshape(v.shape[0] // 8, 8, v.shape[1])

    for ph in range(GROUP + 1):
        score_head = ph if ph < GROUP else None
        value_head = ph - 1 if ph >= 1 else None

        if score_head is not None:
            s_w, sc_w, m_w = s_bufs[ph % 2], sc_bufs[ph % 2], m_bufs[ph % 2]
            qh = q_ref[0, score_head]
            sc = lax.dot_general(kc_ref[0, 0], qh, nt, preferred_element_type=_F32)
            sc_w[...] = sc
            m_w[...] = fold(sc).max(axis=0)
        if value_head is not None:
            s_r, sc_r, m_r = s_bufs[value_head % 2], sc_bufs[value_head % 2], m_bufs[value_head % 2]
            m = m_r[...].max(axis=0, keepdims=True)
            pc = jnp.exp2(sc_r[...] - m)
            l_scr[...] = fold(pc).sum(axis=0)
            acc_scr[...] = jnp.dot(vtc_ref[0, 0, 0], pc.astype(_BF16), preferred_element_type=_F32)

        def chunk(c, carry):
            if score_head is not None:
                s = lax.dot_general(k_ref[0, 0, c], qh, nt, preferred_element_type=_F32)
                s_w[c] = s
                m_w[...] = jnp.maximum(m_w[...], fold(s).max(axis=0))
            if value_head is not None:
                p = jnp.exp2(s_r[c] - m)
                l_scr[...] += fold(p).sum(axis=0)
                acc_scr[...] += jnp.dot(vt_ref[0, 0, c], p.astype(_BF16),
                                        preferred_element_type=_F32)
            return carry

        lax.fori_loop(0, n_chunks, chunk, 0, unroll=True)

        if value_head is not None:
            inv_l = 1.0 / l_scr[...].sum(axis=0, keepdims=True)
            o = (acc_scr[...] * inv_l).T
            cols = slice(value_head * HEAD_DIM, (value_head + 1) * HEAD_DIM)
            o_ref[:, cols] = (o * ga_ref[:, cols].astype(_F32)).astype(_BF16)


def _attention(q, k, vt, k_ctx, vt_ctx, ga, batch, seq):
    n_chunks = seq // KEY_CHUNK
    ctx = k_ctx.shape[2]
    k5 = k.reshape(batch, N_KV_HEADS, n_chunks, KEY_CHUNK, HEAD_DIM)
    q_tiles = seq // Q_TILE
    group_w = GROUP * HEAD_DIM
    score_buf = pltpu.VMEM((n_chunks, KEY_CHUNK, Q_TILE), _F32)
    ctx_score_buf = pltpu.VMEM((ctx, Q_TILE), _F32)
    fold_buf = pltpu.VMEM((8, Q_TILE), _F32)
    return pl.pallas_call(
        _attn_kernel,
        grid=(batch, N_KV_HEADS, q_tiles),
        in_specs=[pl.BlockSpec((1, GROUP, Q_TILE, HEAD_DIM), lambda b, kv, t: (b, kv, t, 0)),
                  pl.BlockSpec((1, 1, n_chunks, KEY_CHUNK, HEAD_DIM), lambda b, kv, t: (b, kv, 0, 0, 0)),
                  pl.BlockSpec((1, 1, n_chunks, HEAD_DIM, KEY_CHUNK), lambda b, kv, t: (b, kv, 0, 0, 0)),
                  pl.BlockSpec((1, 1, ctx, HEAD_DIM), lambda b, kv, t: (b, kv, 0, 0)),
                  pl.BlockSpec((1, 1, 1, HEAD_DIM, ctx), lambda b, kv, t: (b, kv, 0, 0, 0)),
                  pl.BlockSpec((Q_TILE, group_w), lambda b, kv, t: (b * q_tiles + t, kv))],
        out_specs=pl.BlockSpec((Q_TILE, group_w), lambda b, kv, t: (b * q_tiles + t, kv)),
        out_shape=jax.ShapeDtypeStruct((batch * seq, ATTN_WIDTH), _BF16),
        scratch_shapes=[score_buf, score_buf, ctx_score_buf, ctx_score_buf, fold_buf, fold_buf,
                        fold_buf, pltpu.VMEM((HEAD_DIM, Q_TILE), _F32)],
        compiler_params=_params(3),
        name="attention",
    )(q, k5, vt, k_ctx, vt_ctx, ga)


def _out_kernel(tiles_per_seq, attn_ref, u_ref, e_ref, up_ref, un_ref, cw_ref, wa_ref, wc_ref,
                x_ref, gate_ref, gf_ref, o_ref):
    i = pl.program_id(0)
    tm = u_ref.shape[0]
    u = u_ref[...].astype(_F32)
    first = (i % tiles_per_seq) == 0
    last = (i % tiles_per_seq) == tiles_per_seq - 1
    prev_row = jnp.where(first, 0.0, up_ref[HALO_ROWS - 1:HALO_ROWS, :].astype(_F32))
    next_row = jnp.where(last, 0.0, un_ref[0:1, :].astype(_F32))
    rows = lax.broadcasted_iota(jnp.int32, (tm, 1), 0)
    u_before = jnp.where(rows == 0, prev_row, pltpu.roll(u, 1, axis=0))
    u_after = jnp.where(rows == tm - 1, next_row, pltpu.roll(u, tm - 1, axis=0))
    conv = u_before * cw_ref[0:1, :] + u * cw_ref[1:2, :] + u_after * cw_ref[2:3, :]
    cv = (e_ref[...].astype(_F32) * conv).astype(_BF16)
    y = jnp.dot(attn_ref[...], wa_ref[...], preferred_element_type=_F32)
    y += jnp.dot(cv, wc_ref[...], preferred_element_type=_F32)
    xn = x_ref[...] + gate_ref[0] * y
    o_ref[...] = xn * lax.rsqrt(jnp.mean(xn * xn, axis=-1, keepdims=True) + EPS) * gf_ref[...]


def _out_proj(attn, u, e, conv_w, w_out_bf, x2d, gate, gf, seq, tm):
    m, d = x2d.shape
    tiles_per_seq = seq // tm
    halo_per_tile = tm // HALO_ROWS
    n_halo = m // HALO_ROWS
    mix_spec = pl.BlockSpec((tm, ATTN_WIDTH), lambda i: (i, 0))
    return pl.pallas_call(
        functools.partial(_out_kernel, tiles_per_seq),
        grid=(m // tm,),
        in_specs=[mix_spec, mix_spec, mix_spec,
                  pl.BlockSpec((HALO_ROWS, CONV_WIDTH),
                               lambda i: (jnp.maximum(i * halo_per_tile - 1, 0), 0)),
                  pl.BlockSpec((HALO_ROWS, CONV_WIDTH),
                               lambda i: (jnp.minimum((i + 1) * halo_per_tile, n_halo - 1), 0)),
                  pl.BlockSpec((3, CONV_WIDTH), lambda i: (0, 0)),
                  pl.BlockSpec((ATTN_WIDTH, d), lambda i: (0, 0)),
                  pl.BlockSpec((CONV_WIDTH, d), lambda i: (1, 0)),
                  pl.BlockSpec((tm, d), lambda i: (i, 0)),
                  pl.BlockSpec((1, 1, d), lambda i: (i // tiles_per_seq, 0, 0)),
                  pl.BlockSpec((1, d), lambda i: (0, 0))],
        out_specs=pl.BlockSpec((tm, d), lambda i: (i, 0)),
        out_shape=jax.ShapeDtypeStruct((m, d), _F32),
        compiler_params=_params(1),
        name="out_proj",
    )(attn, u, e, u, u, conv_w, w_out_bf, w_out_bf, x2d, gate, gf)


def _rope_tables(seq):
    quarter = HEAD_DIM // 4
    t = jnp.arange(seq, dtype=jnp.int32)
    row = (t // GRID_W).astype(_F32)
    col = (t % GRID_W).astype(_F32)
    inv = ROPE_THETA ** (-jnp.arange(quarter, dtype=_F32) / quarter)
    ang = jnp.concatenate([row[:, None] * inv, row[:, None] * inv,
                           col[:, None] * inv, col[:, None] * inv], axis=1)
    lane = jnp.arange(HEAD_DIM) % (2 * quarter)
    sin = jnp.sin(ang)
    return (jnp.cos(ang), jnp.where(lane < quarter, -sin, 0.0), jnp.where(lane >= quarter, sin, 0.0))


def kernel(x, c, ctx, c_ctx, w_mod, b_mod, norm_g, w_in, q_norm_g, k_norm_g, conv_w, w_out,
           final_norm_g):
    batch, seq, d = x.shape
    ctx_len = ctx.shape[1]
    assert w_mod.shape[0] == 1, "single-layer kernel"
    assert seq % ROW_TILE == 0 and seq % KEY_CHUNK == 0 and seq % Q_TILE == 0

    cond = jnp.concatenate([c, c_ctx[None, :]], axis=0)
    cond_b = jnp.broadcast_to(cond[:, :, None], (batch + 1, d, LANES))
    mod = _adaln(cond_b, w_mod[0], b_mod[0][None, :])
    shift = mod[:batch + 1, None, 0 * d:1 * d]
    scale = mod[:batch + 1, None, 1 * d:2 * d]
    gate = mod[:batch, None, 2 * d:3 * d]

    w_in_bf = w_in[0].astype(_BF16)
    w_out_bf = w_out[0].astype(_BF16)
    g_in = norm_g[0][None, :]
    q_gain = (q_norm_g[0] * (math.log2(math.e) / math.sqrt(HEAD_DIM)))[None, :]
    k_gain = k_norm_g[0][None, :]

    ctx2d = ctx.reshape(batch * ctx_len, d)
    h_ctx = _prenorm(ctx2d, g_in, scale[batch:], shift[batch:], ctx_len, batch)
    no_rope = (jnp.ones((ctx_len, HEAD_DIM), _F32), jnp.zeros((ctx_len, HEAD_DIM), _F32),
               jnp.zeros((ctx_len, HEAD_DIM), _F32))
    k_ctx, vt_ctx = _proj_kv(h_ctx, w_in_bf, k_gain, no_rope, batch, ctx_len, ctx_len)

    x2d = x.reshape(batch * seq, d)
    tables = _rope_tables(seq)
    h, q, k, vt, ga = _proj_attn(x2d, g_in, scale[:batch], shift[:batch], w_in_bf, q_gain, k_gain,
                                 tables, batch, seq, KEY_CHUNK)
    u, e = _proj_conv(h, w_in_bf, ROW_TILE)

    attn = _attention(q, k, vt, k_ctx, vt_ctx, ga, batch, seq)
    out = _out_proj(attn, u, e, conv_w[0], w_out_bf, x2d, gate, final_norm_g[None, :], seq, ROW_TILE)
    return out.reshape(batch, seq, d)
```

```python
import functools
import math

import jax
import jax.numpy as jnp
import numpy as np
from jax import lax
from jax.experimental import pallas as pl
from jax.experimental.pallas import tpu as pltpu

D_MODEL = 2048
CTX_LEN = 256
GRID_W = 64
ATTN_WIDTH = 1024
CONV_WIDTH = 1024
HEAD_DIM = 128
N_HEADS = 8
N_KV_HEADS = 2
GROUP = N_HEADS // N_KV_HEADS
KV_WIDTH = N_KV_HEADS * HEAD_DIM
ROPE_THETA = 10000.0
EPS = 1e-6

LANES = 128
ROW_TILE = 512
COL_TILE = 512
KEY_CHUNK = 512
Q_TILE = 512
HALO_ROWS = 16
MOD_ROWS = 8
MOD_COL_TILE = 768
VMEM_LIMIT = 56 * 1024 * 1024
assert 2 * KV_WIDTH == COL_TILE and ATTN_WIDTH % COL_TILE == 0 and CONV_WIDTH % COL_TILE == 0

_F32 = jnp.float32
_BF16 = jnp.bfloat16


def _silu(v):
    return v * (1.0 / (1.0 + jnp.exp(-v)))


def _params(n_axes):
    return pltpu.CompilerParams(dimension_semantics=("arbitrary",) * n_axes,
                                vmem_limit_bytes=VMEM_LIMIT)


def _adaln_kernel(cb_ref, w_ref, b_ref, o_ref, s_scr):
    @pl.when(pl.program_id(0) == 0)
    def _():
        s_scr[...] = _silu(cb_ref[...])

    o_ref[...] = jnp.zeros(o_ref.shape, _F32)
    for c in range(o_ref.shape[1] // LANES):
        cols = slice(c * LANES, (c + 1) * LANES)
        wc = w_ref[:, cols]
        for r in range(3):
            acc = jnp.sum(wc * s_scr[r], axis=0, keepdims=True)
            o_ref[r:r + 1, cols] = acc + b_ref[:, cols]


def _adaln(cond_b, w_mod, b_mod):
    d, n = w_mod.shape
    return pl.pallas_call(
        _adaln_kernel,
        grid=(n // MOD_COL_TILE,),
        in_specs=[pl.BlockSpec((3, d, LANES), lambda j: (0, 0, 0)),
                  pl.BlockSpec((d, MOD_COL_TILE), lambda j: (0, j)),
                  pl.BlockSpec((1, MOD_COL_TILE), lambda j: (0, j))],
        out_specs=pl.BlockSpec((MOD_ROWS, MOD_COL_TILE), lambda j: (0, j)),
        out_shape=jax.ShapeDtypeStruct((MOD_ROWS, n), _F32),
        scratch_shapes=[pltpu.VMEM((3, d, LANES), _F32)],
        compiler_params=_params(1),
        name="adaln",
    )(cond_b, w_mod, b_mod)


def _prenorm_kernel(x_ref, g_ref, scale_ref, shift_ref, h_ref):
    xf = x_ref[...]
    y = xf * lax.rsqrt(jnp.mean(xf * xf, axis=-1, keepdims=True) + EPS) * g_ref[...]
    h_ref[...] = (y * (1.0 + scale_ref[0]) + shift_ref[0]).astype(_BF16)


def _prenorm(x2d, g, scale, shift, tm, tiles_per_mod):
    m, d = x2d.shape
    mod_spec = pl.BlockSpec((1, 1, d), lambda i: (i // tiles_per_mod, 0, 0))
    return pl.pallas_call(
        _prenorm_kernel,
        grid=(m // tm,),
        in_specs=[pl.BlockSpec((tm, d), lambda i: (i, 0)),
                  pl.BlockSpec((1, d), lambda i: (0, 0)),
                  mod_spec, mod_spec],
        out_specs=pl.BlockSpec((tm, d), lambda i: (i, 0)),
        out_shape=jax.ShapeDtypeStruct((m, d), _BF16),
        compiler_params=_params(1),
        name="prenorm",
    )(x2d, g, scale, shift)


def _norm_rope(xh, g, cos, sin_lo, sin_hi):
    y = xh * lax.rsqrt(jnp.mean(xh * xh, axis=-1, keepdims=True) + EPS) * g
    return (y * cos + pltpu.roll(y, 3 * HEAD_DIM // 4, axis=1) * sin_lo
            + pltpu.roll(y, HEAD_DIM // 4, axis=1) * sin_hi)


def _store_kv(acc, g, cos, slo, shi, k_ref, vt_ref):
    for hh in range(N_KV_HEADS):
        xh = acc[:, hh * HEAD_DIM:(hh + 1) * HEAD_DIM]
        k_ref[0, hh] = _norm_rope(xh, g, cos, slo, shi).astype(_BF16)
        vh = acc[:, KV_WIDTH + hh * HEAD_DIM:KV_WIDTH + (hh + 1) * HEAD_DIM]
        vt_ref[0, hh, 0] = vh.T.astype(_BF16)


def _proj_kv_kernel(h_ref, w_ref, g_ref, cos_ref, slo_ref, shi_ref, k_ref, vt_ref):
    acc = jnp.dot(h_ref[...], w_ref[...].astype(_BF16), preferred_element_type=_F32)
    _store_kv(acc, g_ref[...], cos_ref[...], slo_ref[...], shi_ref[...], k_ref, vt_ref)


def _proj_attn_kernel(x_ref, gin_ref, scale_ref, shift_ref, w_ref, qg_ref, kg_ref,
                      cos_ref, slo_ref, shi_ref, wside_ref, h_ref, q_ref, k_ref, vt_ref, ga_ref,
                      wside_bf_ref):
    wside_bf_ref[...] = wside_ref[...].astype(_BF16)
    xf = x_ref[...]
    y = xf * lax.rsqrt(jnp.mean(xf * xf, axis=-1, keepdims=True) + EPS) * gin_ref[...]
    h = (y * (1.0 + scale_ref[0]) + shift_ref[0]).astype(_BF16)
    h_ref[...] = h
    cos, slo, shi = cos_ref[...], slo_ref[...], shi_ref[...]
    heads_per_dot = COL_TILE // HEAD_DIM

    def project(col0):
        w = w_ref[:, col0:col0 + COL_TILE].astype(_BF16)
        return jnp.dot(h, w, preferred_element_type=_F32)

    for blk in range(ATTN_WIDTH // COL_TILE):
        acc = project(blk * COL_TILE)
        for hh in range(heads_per_dot):
            xh = acc[:, hh * HEAD_DIM:(hh + 1) * HEAD_DIM]
            q_ref[0, blk * heads_per_dot + hh] = _norm_rope(xh, qg_ref[...], cos, slo, shi).astype(_BF16)
    _store_kv(project(ATTN_WIDTH), kg_ref[...], cos, slo, shi, k_ref, vt_ref)
    gate_col = ATTN_WIDTH + 2 * KV_WIDTH
    for blk in range(ATTN_WIDTH // COL_TILE):
        acc = project(gate_col + blk * COL_TILE)
        ga_ref[:, blk * COL_TILE:(blk + 1) * COL_TILE] = _silu(acc).astype(_BF16)


def _proj_conv_kernel(h_ref, wb_ref, wcg_ref, wh_ref, wgc_ref, wside_ref, u_ref, e_ref,
                      wside_bf_ref):
    wside_bf_ref[...] = wside_ref[...].astype(_BF16)
    h = h_ref[...]
    cg = jnp.dot(h, wcg_ref[...], preferred_element_type=_F32)
    hh = jnp.dot(h, wh_ref[...], preferred_element_type=_F32)
    u_ref[...] = (cg * hh).astype(_BF16)
    b = jnp.dot(h, wb_ref[...], preferred_element_type=_F32)
    gc = jnp.dot(h, wgc_ref[...], preferred_element_type=_F32)
    e_ref[...] = (b * _silu(gc)).astype(_BF16)


def _rope_specs(tm, seq):
    tiles = seq // tm
    return [pl.BlockSpec((tm, HEAD_DIM), lambda i, *_: (i % tiles, 0))] * 3


def _proj_attn(x2d, gin, scale, shift, w_in, q_gain, k_gain, tables, batch, seq, tm):
    m, d = x2d.shape
    tiles = seq // tm
    steps = m // tm
    n_cols = 2 * ATTN_WIDTH + 2 * KV_WIDTH
    side_cols = (w_in.shape[1] - n_cols) // steps
    assert side_cols % LANES == 0 and n_cols % side_cols == 0
    side_first = n_cols // side_cols
    mod_spec = pl.BlockSpec((1, 1, d), lambda i: (i // tiles, 0, 0))
    gain_spec = pl.BlockSpec((1, HEAD_DIM), lambda i: (0, 0))
    return pl.pallas_call(
        _proj_attn_kernel,
        grid=(steps,),
        in_specs=[pl.BlockSpec((tm, d), lambda i: (i, 0)),
                  pl.BlockSpec((1, d), lambda i: (0, 0)),
                  mod_spec, mod_spec,
                  pl.BlockSpec((d, n_cols), lambda i: (0, 0)),
                  gain_spec, gain_spec] + _rope_specs(tm, seq)
                 + [pl.BlockSpec((d, side_cols), lambda i: (0, side_first + i))],
        out_specs=[pl.BlockSpec((tm, d), lambda i: (i, 0)),
                   pl.BlockSpec((1, N_HEADS, tm, HEAD_DIM), lambda i: (i // tiles, 0, i % tiles, 0)),
                   pl.BlockSpec((1, N_KV_HEADS, tm, HEAD_DIM), lambda i: (i // tiles, 0, i % tiles, 0)),
                   pl.BlockSpec((1, N_KV_HEADS, 1, HEAD_DIM, tm),
                                lambda i: (i // tiles, 0, i % tiles, 0, 0)),
                   pl.BlockSpec((tm, ATTN_WIDTH), lambda i: (i, 0)),
                   pl.BlockSpec((d, side_cols), lambda i: (0, i))],
        out_shape=[jax.ShapeDtypeStruct((m, d), _BF16),
                   jax.ShapeDtypeStruct((batch, N_HEADS, seq, HEAD_DIM), _BF16),
                   jax.ShapeDtypeStruct((batch, N_KV_HEADS, seq, HEAD_DIM), _BF16),
                   jax.ShapeDtypeStruct((batch, N_KV_HEADS, tiles, HEAD_DIM, tm), _BF16),
                   jax.ShapeDtypeStruct((m, ATTN_WIDTH), _BF16),
                   jax.ShapeDtypeStruct((d, side_cols * steps), _BF16)],
        compiler_params=_params(1),
        name="proj_attn",
    )(x2d, gin, scale, shift, w_in, q_gain, k_gain, *tables, w_in)


def _proj_kv(h, w_bf, g, tables, batch, seq, tm):
    m, d = h.shape
    tiles = seq // tm
    kv_col_block = ATTN_WIDTH // COL_TILE
    return pl.pallas_call(
        _proj_kv_kernel,
        grid=(m // tm,),
        in_specs=[pl.BlockSpec((tm, d), lambda i: (i, 0)),
                  pl.BlockSpec((d, 2 * KV_WIDTH), lambda i: (0, kv_col_block)),
                  pl.BlockSpec((1, HEAD_DIM), lambda i: (0, 0))] + _rope_specs(tm, seq),
        out_specs=[pl.BlockSpec((1, N_KV_HEADS, tm, HEAD_DIM), lambda i: (i // tiles, 0, i % tiles, 0)),
                   pl.BlockSpec((1, N_KV_HEADS, 1, HEAD_DIM, tm),
                                lambda i: (i // tiles, 0, i % tiles, 0, 0))],
        out_shape=[jax.ShapeDtypeStruct((batch, N_KV_HEADS, seq, HEAD_DIM), _BF16),
                   jax.ShapeDtypeStruct((batch, N_KV_HEADS, tiles, HEAD_DIM, tm), _BF16)],
        compiler_params=_params(1),
        name="proj_kv",
    )(h, w_bf, g, *tables)


def _proj_conv(h, w_conv_bf, w_out, tm):
    m, d = h.shape
    step = CONV_WIDTH // COL_TILE
    side_rows = w_out.shape[0] // (m // tm)
    assert side_rows % HALO_ROWS == 0

    def w_spec(group):
        return pl.BlockSpec((d, COL_TILE), lambda i, j: (0, group * step + j))

    out_spec = pl.BlockSpec((tm, COL_TILE), lambda i, j: (i, j))
    side_spec = pl.BlockSpec((side_rows, w_out.shape[1]), lambda i, j: (i, 0))
    return pl.pallas_call(
        _proj_conv_kernel,
        grid=(m // tm, step),
        in_specs=[pl.BlockSpec((tm, d), lambda i, j: (i, 0)),
                  w_spec(0), w_spec(1), w_spec(2), w_spec(3), side_spec],
        out_specs=[out_spec, out_spec, side_spec],
        out_shape=[jax.ShapeDtypeStruct((m, CONV_WIDTH), _BF16)] * 2
                  + [jax.ShapeDtypeStruct(w_out.shape, _BF16)],
        compiler_params=_params(2),
        name="proj_conv",
    )(h, w_conv_bf, w_conv_bf, w_conv_bf, w_conv_bf, w_out)


def _attn_kernel(q_ref, k_ref, vt_ref, kc_ref, vtc_ref, ga_ref, o_ref,
                 s_even, s_odd, sc_even, sc_odd, m_even, m_odd, l_scr, acc_scr):
    n_chunks = k_ref.shape[2]
    nt = (((1,), (1,)), ((), ()))
    s_bufs, sc_bufs, m_bufs = (s_even, s_odd), (sc_even, sc_odd), (m_even, m_odd)

    def fold(v):
        return v.reshape(v.shape[0] // 8, 8, v.shape[1])

    for ph in range(GROUP + 1):
        score_head = ph if ph < GROUP else None
        value_head = ph - 1 if ph >= 1 else None

        if score_head is not None:
            s_w, sc_w, m_w = s_bufs[ph % 2], sc_bufs[ph % 2], m_bufs[ph % 2]
            qh = q_ref[0, score_head]
            sc = lax.dot_general(kc_ref[0, 0], qh, nt, preferred_element_type=_F32)
            sc_w[...] = sc
            m_w[...] = fold(sc).max(axis=0)
        if value_head is not None:
            s_r, sc_r, m_r = s_bufs[value_head % 2], sc_bufs[value_head % 2], m_bufs[value_head % 2]
            m = m_r[...].max(axis=0, keepdims=True)
            pc = jnp.exp2(sc_r[...] - m)
            l_scr[...] = fold(pc).sum(axis=0)
            acc_scr[...] = jnp.dot(vtc_ref[0, 0, 0], pc.astype(_BF16), preferred_element_type=_F32)

        def chunk(c, carry):
            if score_head is not None:
                s = lax.dot_general(k_ref[0, 0, c], qh, nt, preferred_element_type=_F32)
                s_w[c] = s
                m_w[...] = jnp.maximum(m_w[...], fold(s).max(axis=0))
            if value_head is not None:
                p = jnp.exp2(s_r[c] - m)
                l_scr[...] += fold(p).sum(axis=0)
                acc_scr[...] += jnp.dot(vt_ref[0, 0, c], p.astype(_BF16),
                                        preferred_element_type=_F32)
            return carry

        lax.fori_loop(0, n_chunks, chunk, 0, unroll=True)

        if value_head is not None:
            inv_l = 1.0 / l_scr[...].sum(axis=0, keepdims=True)
            o = (acc_scr[...] * inv_l).T
            cols = slice(value_head * HEAD_DIM, (value_head + 1) * HEAD_DIM)
            o_ref[:, cols] = (o * ga_ref[:, cols].astype(_F32)).astype(_BF16)


def _attention(q, k, vt, k_ctx, vt_ctx, ga, batch, seq):
    n_chunks = seq // KEY_CHUNK
    ctx = k_ctx.shape[2]
    k5 = k.reshape(batch, N_KV_HEADS, n_chunks, KEY_CHUNK, HEAD_DIM)
    q_tiles = seq // Q_TILE
    group_w = GROUP * HEAD_DIM
    score_buf = pltpu.VMEM((n_chunks, KEY_CHUNK, Q_TILE), _F32)
    ctx_score_buf = pltpu.VMEM((ctx, Q_TILE), _F32)
    fold_buf = pltpu.VMEM((8, Q_TILE), _F32)
    return pl.pallas_call(
        _attn_kernel,
        grid=(batch, N_KV_HEADS, q_tiles),
        in_specs=[pl.BlockSpec((1, GROUP, Q_TILE, HEAD_DIM), lambda b, kv, t: (b, kv, t, 0)),
                  pl.BlockSpec((1, 1, n_chunks, KEY_CHUNK, HEAD_DIM), lambda b, kv, t: (b, kv, 0, 0, 0)),
                  pl.BlockSpec((1, 1, n_chunks, HEAD_DIM, KEY_CHUNK), lambda b, kv, t: (b, kv, 0, 0, 0)),
                  pl.BlockSpec((1, 1, ctx, HEAD_DIM), lambda b, kv, t: (b, kv, 0, 0)),
                  pl.BlockSpec((1, 1, 1, HEAD_DIM, ctx), lambda b, kv, t: (b, kv, 0, 0, 0)),
                  pl.BlockSpec((Q_TILE, group_w), lambda b, kv, t: (b * q_tiles + t, kv))],
        out_specs=pl.BlockSpec((Q_TILE, group_w), lambda b, kv, t: (b * q_tiles + t, kv)),
        out_shape=jax.ShapeDtypeStruct((batch * seq, ATTN_WIDTH), _BF16),
        scratch_shapes=[score_buf, score_buf, ctx_score_buf, ctx_score_buf, fold_buf, fold_buf,
                        fold_buf, pltpu.VMEM((HEAD_DIM, Q_TILE), _F32)],
        compiler_params=_params(3),
        name="attention",
    )(q, k5, vt, k_ctx, vt_ctx, ga)


def _out_kernel(tiles_per_seq, attn_ref, u_ref, e_ref, up_ref, un_ref, cw_ref, wa_ref, wc_ref,
                x_ref, gate_ref, gf_ref, o_ref):
    i = pl.program_id(0)
    tm = u_ref.shape[0]
    u = u_ref[...].astype(_F32)
    first = (i % tiles_per_seq) == 0
    last = (i % tiles_per_seq) == tiles_per_seq - 1
    prev_row = jnp.where(first, 0.0, up_ref[HALO_ROWS - 1:HALO_ROWS, :].astype(_F32))
    next_row = jnp.where(last, 0.0, un_ref[0:1, :].astype(_F32))
    rows = lax.broadcasted_iota(jnp.int32, (tm, 1), 0)
    u_before = jnp.where(rows == 0, prev_row, pltpu.roll(u, 1, axis=0))
    u_after = jnp.where(rows == tm - 1, next_row, pltpu.roll(u, tm - 1, axis=0))
    conv = u_before * cw_ref[0:1, :] + u * cw_ref[1:2, :] + u_after * cw_ref[2:3, :]
    cv = (e_ref[...].astype(_F32) * conv).astype(_BF16)
    y = jnp.dot(attn_ref[...], wa_ref[...], preferred_element_type=_F32)
    y += jnp.dot(cv, wc_ref[...], preferred_element_type=_F32)
    xn = x_ref[...] + gate_ref[0] * y
    o_ref[...] = xn * lax.rsqrt(jnp.mean(xn * xn, axis=-1, keepdims=True) + EPS) * gf_ref[...]


def _out_proj(attn, u, e, conv_w, w_out_bf, x2d, gate, gf, seq, tm):
    m, d = x2d.shape
    tiles_per_seq = seq // tm
    halo_per_tile = tm // HALO_ROWS
    n_halo = m // HALO_ROWS
    mix_spec = pl.BlockSpec((tm, ATTN_WIDTH), lambda i: (i, 0))
    return pl.pallas_call(
        functools.partial(_out_kernel, tiles_per_seq),
        grid=(m // tm,),
        in_specs=[mix_spec, mix_spec, mix_spec,
                  pl.BlockSpec((HALO_ROWS, CONV_WIDTH),
                               lambda i: (jnp.maximum(i * halo_per_tile - 1, 0), 0)),
                  pl.BlockSpec((HALO_ROWS, CONV_WIDTH),
                               lambda i: (jnp.minimum((i + 1) * halo_per_tile, n_halo - 1), 0)),
                  pl.BlockSpec((3, CONV_WIDTH), lambda i: (0, 0)),
                  pl.BlockSpec((ATTN_WIDTH, d), lambda i: (0, 0)),
                  pl.BlockSpec((CONV_WIDTH, d), lambda i: (1, 0)),
                  pl.BlockSpec((tm, d), lambda i: (i, 0)),
                  pl.BlockSpec((1, 1, d), lambda i: (i // tiles_per_seq, 0, 0)),
                  pl.BlockSpec((1, d), lambda i: (0, 0))],
        out_specs=pl.BlockSpec((tm, d), lambda i: (i, 0)),
        out_shape=jax.ShapeDtypeStruct((m, d), _F32),
        compiler_params=_params(1),
        name="out_proj",
    )(attn, u, e, u, u, conv_w, w_out_bf, w_out_bf, x2d, gate, gf)


def _rope_tables(seq):
    quarter = HEAD_DIM // 4
    t = np.arange(seq)
    row = (t // GRID_W).astype(np.float64)
    col = (t % GRID_W).astype(np.float64)
    inv = ROPE_THETA ** (-np.arange(quarter, dtype=np.float64) / quarter)
    ang = np.concatenate([row[:, None] * inv, row[:, None] * inv,
                          col[:, None] * inv, col[:, None] * inv], axis=1)
    lane = np.arange(HEAD_DIM) % (2 * quarter)
    sin = np.sin(ang)
    tables = (np.cos(ang), np.where(lane < quarter, -sin, 0.0), np.where(lane >= quarter, sin, 0.0))
    return tuple(jnp.asarray(tab.astype(np.float32)) for tab in tables)


def kernel(x, c, ctx, c_ctx, w_mod, b_mod, norm_g, w_in, q_norm_g, k_norm_g, conv_w, w_out,
           final_norm_g):
    batch, seq, d = x.shape
    ctx_len = ctx.shape[1]
    assert w_mod.shape[0] == 1, "single-layer kernel"
    assert seq % ROW_TILE == 0 and seq % KEY_CHUNK == 0 and seq % Q_TILE == 0

    cond = jnp.concatenate([c, c_ctx[None, :]], axis=0)
    cond_b = jnp.broadcast_to(cond[:, :, None], (batch + 1, d, LANES))
    mod = _adaln(cond_b, w_mod[0], b_mod[0][None, :])
    shift = mod[:batch + 1, None, 0 * d:1 * d]
    scale = mod[:batch + 1, None, 1 * d:2 * d]
    gate = mod[:batch, None, 2 * d:3 * d]

    g_in = norm_g[0][None, :]
    q_gain = (q_norm_g[0] * (math.log2(math.e) / math.sqrt(HEAD_DIM)))[None, :]
    k_gain = k_norm_g[0][None, :]

    ctx2d = ctx.reshape(batch * ctx_len, d)
    h_ctx = _prenorm(ctx2d, g_in, scale[batch:], shift[batch:], ctx_len, batch)
    no_rope = (jnp.ones((ctx_len, HEAD_DIM), _F32), jnp.zeros((ctx_len, HEAD_DIM), _F32),
               jnp.zeros((ctx_len, HEAD_DIM), _F32))
    k_ctx, vt_ctx = _proj_kv(h_ctx, w_in[0], k_gain, no_rope, batch, ctx_len, ctx_len)

    x2d = x.reshape(batch * seq, d)
    tables = _rope_tables(seq)
    h, q, k, vt, ga, w_conv_bf = _proj_attn(x2d, g_in, scale[:batch], shift[:batch], w_in[0],
                                            q_gain, k_gain, tables, batch, seq, KEY_CHUNK)
    u, e, w_out_bf = _proj_conv(h, w_conv_bf, w_out[0], ROW_TILE)

    attn = _attention(q, k, vt, k_ctx, vt_ctx, ga, batch, seq)
    out = _out_proj(attn, u, e, conv_w[0], w_out_bf, x2d, gate, final_norm_g[None, :], seq, ROW_TILE)
    return out.reshape(batch, seq, d)
```

```python
import functools
import math

import jax
import jax.numpy as jnp
import numpy as np
from jax import lax
from jax.experimental import pallas as pl
from jax.experimental.pallas import tpu as pltpu

D_MODEL = 2048
CTX_LEN = 256
GRID_W = 64
ATTN_WIDTH = 1024
CONV_WIDTH = 1024
HEAD_DIM = 128
N_HEADS = 8
N_KV_HEADS = 2
GROUP = N_HEADS // N_KV_HEADS
KV_WIDTH = N_KV_HEADS * HEAD_DIM
ROPE_THETA = 10000.0
EPS = 1e-6

LANES = 128
ROW_TILE = 512
COL_TILE = 512
KEY_CHUNK = 512
Q_TILE = 512
V_ROWS = HEAD_DIM + 16
HALO_ROWS = 16
MOD_ROWS = 8
MOD_COL_TILE = 768
VMEM_LIMIT = 56 * 1024 * 1024
assert 2 * KV_WIDTH == COL_TILE and ATTN_WIDTH % COL_TILE == 0 and CONV_WIDTH % COL_TILE == 0

_F32 = jnp.float32
_BF16 = jnp.bfloat16


def _silu(v):
    return v * (1.0 / (1.0 + jnp.exp(-v)))


def _params(n_axes, flags=None):
    return pltpu.CompilerParams(dimension_semantics=("arbitrary",) * n_axes,
                                vmem_limit_bytes=VMEM_LIMIT, flags=flags)


def _adaln_kernel(cb_ref, w_ref, b_ref, o_ref, s_scr):
    @pl.when(pl.program_id(0) == 0)
    def _():
        s_scr[...] = _silu(cb_ref[...])

    o_ref[...] = jnp.zeros(o_ref.shape, _F32)
    for c in range(o_ref.shape[1] // LANES):
        cols = slice(c * LANES, (c + 1) * LANES)
        wc = w_ref[:, cols]
        for r in range(3):
            acc = jnp.sum(wc * s_scr[r], axis=0, keepdims=True)
            o_ref[r:r + 1, cols] = acc + b_ref[:, cols]


def _adaln(cond_b, w_mod, b_mod):
    d, n = w_mod.shape
    return pl.pallas_call(
        _adaln_kernel,
        grid=(n // MOD_COL_TILE,),
        in_specs=[pl.BlockSpec((3, d, LANES), lambda j: (0, 0, 0)),
                  pl.BlockSpec((d, MOD_COL_TILE), lambda j: (0, j)),
                  pl.BlockSpec((1, MOD_COL_TILE), lambda j: (0, j))],
        out_specs=pl.BlockSpec((MOD_ROWS, MOD_COL_TILE), lambda j: (0, j)),
        out_shape=jax.ShapeDtypeStruct((MOD_ROWS, n), _F32),
        scratch_shapes=[pltpu.VMEM((3, d, LANES), _F32)],
        compiler_params=_params(1),
        name="adaln",
    )(cond_b, w_mod, b_mod)


def _prenorm_kernel(x_ref, g_ref, scale_ref, shift_ref, h_ref):
    xf = x_ref[...]
    y = xf * lax.rsqrt(jnp.mean(xf * xf, axis=-1, keepdims=True) + EPS) * g_ref[...]
    h_ref[...] = (y * (1.0 + scale_ref[0]) + shift_ref[0]).astype(_BF16)


def _prenorm(x2d, g, scale, shift, tm, tiles_per_mod):
    m, d = x2d.shape
    mod_spec = pl.BlockSpec((1, 1, d), lambda i: (i // tiles_per_mod, 0, 0))
    return pl.pallas_call(
        _prenorm_kernel,
        grid=(m // tm,),
        in_specs=[pl.BlockSpec((tm, d), lambda i: (i, 0)),
                  pl.BlockSpec((1, d), lambda i: (0, 0)),
                  mod_spec, mod_spec],
        out_specs=pl.BlockSpec((tm, d), lambda i: (i, 0)),
        out_shape=jax.ShapeDtypeStruct((m, d), _BF16),
        compiler_params=_params(1),
        name="prenorm",
    )(x2d, g, scale, shift)


def _norm_rope(xh, g, cos, sin_lo, sin_hi):
    y = xh * lax.rsqrt(jnp.mean(xh * xh, axis=-1, keepdims=True) + EPS) * g
    return (y * cos + pltpu.roll(y, 3 * HEAD_DIM // 4, axis=1) * sin_lo
            + pltpu.roll(y, HEAD_DIM // 4, axis=1) * sin_hi)


def _store_kv(acc, g, cos, slo, shi, k_ref, vt_ref):
    for hh in range(N_KV_HEADS):
        xh = acc[:, hh * HEAD_DIM:(hh + 1) * HEAD_DIM]
        k_ref[0, hh] = _norm_rope(xh, g, cos, slo, shi).astype(_BF16)
        vh = acc[:, KV_WIDTH + hh * HEAD_DIM:KV_WIDTH + (hh + 1) * HEAD_DIM]
        vt_ref[0, hh, 0:HEAD_DIM] = vh.T.astype(_BF16)
        vt_ref[0, hh, HEAD_DIM:V_ROWS] = jnp.ones((V_ROWS - HEAD_DIM, vh.shape[0]), _BF16)


def _proj_kv_kernel(h_ref, w_ref, g_ref, cos_ref, slo_ref, shi_ref, k_ref, vt_ref):
    acc = jnp.dot(h_ref[...], w_ref[...].astype(_BF16), preferred_element_type=_F32)
    _store_kv(acc, g_ref[...], cos_ref[...], slo_ref[...], shi_ref[...], k_ref, vt_ref)


def _proj_attn_kernel(x_ref, gin_ref, scale_ref, shift_ref, w_ref, qg_ref, kg_ref,
                      cos_ref, slo_ref, shi_ref, wside_ref, h_ref, q_ref, k_ref, vt_ref, ga_ref,
                      wside_bf_ref):
    wside_bf_ref[...] = wside_ref[...].astype(_BF16)
    xf = x_ref[...]
    y = xf * lax.rsqrt(jnp.mean(xf * xf, axis=-1, keepdims=True) + EPS) * gin_ref[...]
    h = (y * (1.0 + scale_ref[0]) + shift_ref[0]).astype(_BF16)
    h_ref[...] = h
    cos, slo, shi = cos_ref[...], slo_ref[...], shi_ref[...]
    heads_per_dot = COL_TILE // HEAD_DIM

    def project(col0):
        w = w_ref[:, col0:col0 + COL_TILE].astype(_BF16)
        return jnp.dot(h, w, preferred_element_type=_F32)

    for blk in range(ATTN_WIDTH // COL_TILE):
        acc = project(blk * COL_TILE)
        for hh in range(heads_per_dot):
            xh = acc[:, hh * HEAD_DIM:(hh + 1) * HEAD_DIM]
            q_ref[0, blk * heads_per_dot + hh] = _norm_rope(xh, qg_ref[...], cos, slo, shi).astype(_BF16)
    _store_kv(project(ATTN_WIDTH), kg_ref[...], cos, slo, shi, k_ref, vt_ref)
    gate_col = ATTN_WIDTH + 2 * KV_WIDTH
    for blk in range(ATTN_WIDTH // COL_TILE):
        acc = project(gate_col + blk * COL_TILE)
        ga_ref[:, blk * COL_TILE:(blk + 1) * COL_TILE] = _silu(acc).astype(_BF16)


def _proj_conv_kernel(h_ref, wb_ref, wcg_ref, wh_ref, wgc_ref, wside_ref, u_ref, e_ref,
                      wside_bf_ref):
    wside_bf_ref[...] = wside_ref[...].astype(_BF16)
    h = h_ref[...]
    cg = jnp.dot(h, wcg_ref[...], preferred_element_type=_F32)
    hh = jnp.dot(h, wh_ref[...], preferred_element_type=_F32)
    u_ref[...] = (cg * hh).astype(_BF16)
    b = jnp.dot(h, wb_ref[...], preferred_element_type=_F32)
    gc = jnp.dot(h, wgc_ref[...], preferred_element_type=_F32)
    e_ref[...] = (b * _silu(gc)).astype(_BF16)


def _rope_specs(tm, seq):
    tiles = seq // tm
    return [pl.BlockSpec((tm, HEAD_DIM), lambda i, *_: (i % tiles, 0))] * 3


def _proj_attn(x2d, gin, scale, shift, w_in, q_gain, k_gain, tables, batch, seq, tm):
    m, d = x2d.shape
    tiles = seq // tm
    steps = m // tm
    n_cols = 2 * ATTN_WIDTH + 2 * KV_WIDTH
    side_cols = (w_in.shape[1] - n_cols) // steps
    assert side_cols % LANES == 0 and n_cols % side_cols == 0
    side_first = n_cols // side_cols
    mod_spec = pl.BlockSpec((1, 1, d), lambda i: (i // tiles, 0, 0))
    gain_spec = pl.BlockSpec((1, HEAD_DIM), lambda i: (0, 0))
    return pl.pallas_call(
        _proj_attn_kernel,
        grid=(steps,),
        in_specs=[pl.BlockSpec((tm, d), lambda i: (i, 0)),
                  pl.BlockSpec((1, d), lambda i: (0, 0)),
                  mod_spec, mod_spec,
                  pl.BlockSpec((d, n_cols), lambda i: (0, 0)),
                  gain_spec, gain_spec] + _rope_specs(tm, seq)
                 + [pl.BlockSpec((d, side_cols), lambda i: (0, side_first + i))],
        out_specs=[pl.BlockSpec((tm, d), lambda i: (i, 0)),
                   pl.BlockSpec((1, N_HEADS, tm, HEAD_DIM), lambda i: (i // tiles, 0, i % tiles, 0)),
                   pl.BlockSpec((1, N_KV_HEADS, tm, HEAD_DIM), lambda i: (i // tiles, 0, i % tiles, 0)),
                   pl.BlockSpec((1, N_KV_HEADS, V_ROWS, tm), lambda i: (i // tiles, 0, 0, i % tiles)),
                   pl.BlockSpec((tm, ATTN_WIDTH), lambda i: (i, 0)),
                   pl.BlockSpec((d, side_cols), lambda i: (0, i))],
        out_shape=[jax.ShapeDtypeStruct((m, d), _BF16),
                   jax.ShapeDtypeStruct((batch, N_HEADS, seq, HEAD_DIM), _BF16),
                   jax.ShapeDtypeStruct((batch, N_KV_HEADS, seq, HEAD_DIM), _BF16),
                   jax.ShapeDtypeStruct((batch, N_KV_HEADS, V_ROWS, seq), _BF16),
                   jax.ShapeDtypeStruct((m, ATTN_WIDTH), _BF16),
                   jax.ShapeDtypeStruct((d, side_cols * steps), _BF16)],
        compiler_params=_params(1),
        name="proj_attn",
    )(x2d, gin, scale, shift, w_in, q_gain, k_gain, *tables, w_in)


def _proj_kv(h, w_in, g, tables, batch, seq, tm):
    m, d = h.shape
    tiles = seq // tm
    kv_col_block = ATTN_WIDTH // COL_TILE
    return pl.pallas_call(
        _proj_kv_kernel,
        grid=(m // tm,),
        in_specs=[pl.BlockSpec((tm, d), lambda i: (i, 0)),
                  pl.BlockSpec((d, 2 * KV_WIDTH), lambda i: (0, kv_col_block)),
                  pl.BlockSpec((1, HEAD_DIM), lambda i: (0, 0))] + _rope_specs(tm, seq),
        out_specs=[pl.BlockSpec((1, N_KV_HEADS, tm, HEAD_DIM), lambda i: (i // tiles, 0, i % tiles, 0)),
                   pl.BlockSpec((1, N_KV_HEADS, V_ROWS, tm), lambda i: (i // tiles, 0, 0, i % tiles))],
        out_shape=[jax.ShapeDtypeStruct((batch, N_KV_HEADS, seq, HEAD_DIM), _BF16),
                   jax.ShapeDtypeStruct((batch, N_KV_HEADS, V_ROWS, seq), _BF16)],
        compiler_params=_params(1),
        name="proj_kv",
    )(h, w_in, g, *tables)


def _proj_conv(h, w_conv_bf, w_out, tm):
    m, d = h.shape
    step = CONV_WIDTH // COL_TILE
    side_rows = w_out.shape[0] // (m // tm)
    assert side_rows % HALO_ROWS == 0

    def w_spec(group):
        return pl.BlockSpec((d, COL_TILE), lambda i, j: (0, group * step + j))

    out_spec = pl.BlockSpec((tm, COL_TILE), lambda i, j: (i, j))
    side_spec = pl.BlockSpec((side_rows, w_out.shape[1]), lambda i, j: (i, 0))
    return pl.pallas_call(
        _proj_conv_kernel,
        grid=(m // tm, step),
        in_specs=[pl.BlockSpec((tm, d), lambda i, j: (i, 0)),
                  w_spec(0), w_spec(1), w_spec(2), w_spec(3), side_spec],
        out_specs=[out_spec, out_spec, side_spec],
        out_shape=[jax.ShapeDtypeStruct((m, CONV_WIDTH), _BF16)] * 2
                  + [jax.ShapeDtypeStruct(w_out.shape, _BF16)],
        compiler_params=_params(2),
        name="proj_conv",
    )(h, w_conv_bf, w_conv_bf, w_conv_bf, w_conv_bf, w_out)


def _attn_kernel(q_ref, k_ref, vt_ref, kc_ref, vtc_ref, ga_ref, o_ref, s_ring, m_ring, p_ring):
    n_chunks, kc = k_ref.shape[2], k_ref.shape[3]
    n_lat, ctx = n_chunks * kc, kc_ref.shape[2]
    tq = s_ring[0].shape[1]
    n_tiles = q_ref.shape[2] // tq
    nt = (((1,), (1,)), ((), ()))

    def fold(v):
        return v.reshape(v.shape[0] // 8, 8, v.shape[1])

    def rows(t):
        return pl.ds(pl.multiple_of(t * tq, tq), tq)

    def key_block(c):
        return (k_ref[0, 0, c], pl.ds(c * kc, kc)) if c < n_chunks else (kc_ref[0, 0], pl.ds(n_lat, ctx))

    def scores(t, g, c):
        keys, span = key_block(c)
        s = lax.dot_general(keys, q_ref[0, g, rows(t), :], nt, preferred_element_type=_F32)
        s_ring[g % 2][span, :] = s
        chunk_max = fold(s).max(axis=0)
        m_buf = m_ring[g % 2]
        m_buf[...] = chunk_max if c == 0 else jnp.maximum(m_buf[...], chunk_max)

    def probs(g, c, m):
        _, span = key_block(c)
        p_ring[g % 2][span, :] = jnp.exp2(s_ring[g % 2][span, :] - m).astype(_BF16)

    def values(t, g):
        p_buf = p_ring[g % 2]
        acc = jnp.dot(vt_ref[0, 0], p_buf[0:n_lat, :], preferred_element_type=_F32)
        acc += jnp.dot(vtc_ref[0, 0], p_buf[n_lat:n_lat + ctx, :], preferred_element_type=_F32)
        o = (acc[0:HEAD_DIM] * (1.0 / acc[HEAD_DIM:HEAD_DIM + 1])).T
        cols = slice(g * HEAD_DIM, (g + 1) * HEAD_DIM)
        o_ref[rows(t), cols] = (o * ga_ref[rows(t), cols].astype(_F32)).astype(_BF16)

    def phase(score_item, prob_head, value_item):
        if prob_head is not None:
            m = m_ring[prob_head % 2][...].max(axis=0, keepdims=True)
        for c in range(n_chunks + 1):
            if score_item is not None:
                scores(*score_item, c)
            if prob_head is not None:
                probs(prob_head, c, m)
        if value_item is not None:
            values(*value_item)

    assert GROUP == 4
    phase((0, 0), None, None)
    phase((0, 1), 0, None)

    def tile(t, carry):
        nxt = jnp.minimum(t + 1, n_tiles - 1)
        phase((t, 2), 1, (t, 0))
        phase((t, 3), 2, (t, 1))
        phase((nxt, 0), 3, (t, 2))
        phase((nxt, 1), 0, (t, 3))
        return carry

    lax.fori_loop(0, n_tiles, tile, 0)


def _attention(q, k, vt, k_ctx, vt_ctx, ga, batch, seq):
    n_chunks = seq // KEY_CHUNK
    ctx = k_ctx.shape[2]
    k5 = k.reshape(batch, N_KV_HEADS, n_chunks, KEY_CHUNK, HEAD_DIM)
    group_w = GROUP * HEAD_DIM
    n_keys = seq + ctx
    return pl.pallas_call(
        _attn_kernel,
        grid=(batch, N_KV_HEADS),
        in_specs=[pl.BlockSpec((1, GROUP, seq, HEAD_DIM), lambda b, kv: (b, kv, 0, 0),
                               pipeline_mode=pl.Buffered(1)),
                  pl.BlockSpec((1, 1, n_chunks, KEY_CHUNK, HEAD_DIM), lambda b, kv: (b, kv, 0, 0, 0)),
                  pl.BlockSpec((1, 1, V_ROWS, seq), lambda b, kv: (b, kv, 0, 0)),
                  pl.BlockSpec((1, 1, ctx, HEAD_DIM), lambda b, kv: (b, kv, 0, 0)),
                  pl.BlockSpec((1, 1, V_ROWS, ctx), lambda b, kv: (b, kv, 0, 0)),
                  pl.BlockSpec((seq, group_w), lambda b, kv: (b, kv), pipeline_mode=pl.Buffered(1))],
        out_specs=pl.BlockSpec((seq, group_w), lambda b, kv: (b, kv)),
        out_shape=jax.ShapeDtypeStruct((batch * seq, ATTN_WIDTH), _BF16),
        scratch_shapes=[[pltpu.VMEM((n_keys, Q_TILE), _F32)] * 2,
                        [pltpu.VMEM((8, Q_TILE), _F32)] * 2,
                        [pltpu.VMEM((n_keys, Q_TILE), _BF16)] * 2],
        compiler_params=_params(2),
        name="attention",
    )(q, k5, vt, k_ctx, vt_ctx, ga)


def _out_kernel(tiles_per_seq, attn_ref, u_ref, e_ref, up_ref, un_ref, cw_ref, wa_ref, wc_ref,
                x_ref, gate_ref, gf_ref, o_ref):
    i = pl.program_id(0)
    tm = u_ref.shape[0]
    u = u_ref[...].astype(_F32)
    first = (i % tiles_per_seq) == 0
    last = (i % tiles_per_seq) == tiles_per_seq - 1
    prev_row = jnp.where(first, 0.0, up_ref[HALO_ROWS - 1:HALO_ROWS, :].astype(_F32))
    next_row = jnp.where(last, 0.0, un_ref[0:1, :].astype(_F32))
    rows = lax.broadcasted_iota(jnp.int32, (tm, 1), 0)
    u_before = jnp.where(rows == 0, prev_row, pltpu.roll(u, 1, axis=0))
    u_after = jnp.where(rows == tm - 1, next_row, pltpu.roll(u, tm - 1, axis=0))
    conv = u_before * cw_ref[0:1, :] + u * cw_ref[1:2, :] + u_after * cw_ref[2:3, :]
    cv = (e_ref[...].astype(_F32) * conv).astype(_BF16)
    y = jnp.dot(attn_ref[...], wa_ref[...], preferred_element_type=_F32)
    y += jnp.dot(cv, wc_ref[...], preferred_element_type=_F32)
    xn = x_ref[...] + gate_ref[0] * y
    o_ref[...] = xn * lax.rsqrt(jnp.mean(xn * xn, axis=-1, keepdims=True) + EPS) * gf_ref[...]


def _out_proj(attn, u, e, conv_w, w_out_bf, x2d, gate, gf, seq, tm):
    m, d = x2d.shape
    tiles_per_seq = seq // tm
    halo_per_tile = tm // HALO_ROWS
    n_halo = m // HALO_ROWS
    mix_spec = pl.BlockSpec((tm, ATTN_WIDTH), lambda i: (i, 0))
    return pl.pallas_call(
        functools.partial(_out_kernel, tiles_per_seq),
        grid=(m // tm,),
        in_specs=[mix_spec, mix_spec, mix_spec,
                  pl.BlockSpec((HALO_ROWS, CONV_WIDTH),
                               lambda i: (jnp.maximum(i * halo_per_tile - 1, 0), 0)),
                  pl.BlockSpec((HALO_ROWS, CONV_WIDTH),
                               lambda i: (jnp.minimum((i + 1) * halo_per_tile, n_halo - 1), 0)),
                  pl.BlockSpec((3, CONV_WIDTH), lambda i: (0, 0)),
                  pl.BlockSpec((ATTN_WIDTH, d), lambda i: (0, 0)),
                  pl.BlockSpec((CONV_WIDTH, d), lambda i: (1, 0)),
                  pl.BlockSpec((tm, d), lambda i: (i, 0)),
                  pl.BlockSpec((1, 1, d), lambda i: (i // tiles_per_seq, 0, 0)),
                  pl.BlockSpec((1, d), lambda i: (0, 0))],
        out_specs=pl.BlockSpec((tm, d), lambda i: (i, 0)),
        out_shape=jax.ShapeDtypeStruct((m, d), _F32),
        compiler_params=_params(1),
        name="out_proj",
    )(attn, u, e, u, u, conv_w, w_out_bf, w_out_bf, x2d, gate, gf)


def _rope_tables(seq):
    quarter = HEAD_DIM // 4
    t = np.arange(seq)
    row = (t // GRID_W).astype(np.float64)
    col = (t % GRID_W).astype(np.float64)
    inv = ROPE_THETA ** (-np.arange(quarter, dtype=np.float64) / quarter)
    ang = np.concatenate([row[:, None] * inv, row[:, None] * inv,
                          col[:, None] * inv, col[:, None] * inv], axis=1)
    lane = np.arange(HEAD_DIM) % (2 * quarter)
    sin = np.sin(ang)
    tables = (np.cos(ang), np.where(lane < quarter, -sin, 0.0), np.where(lane >= quarter, sin, 0.0))
    return tuple(jnp.asarray(tab.astype(np.float32)) for tab in tables)


def kernel(x, c, ctx, c_ctx, w_mod, b_mod, norm_g, w_in, q_norm_g, k_norm_g, conv_w, w_out,
           final_norm_g):
    batch, seq, d = x.shape
    ctx_len = ctx.shape[1]
    assert w_mod.shape[0] == 1, "single-layer kernel"
    assert seq % ROW_TILE == 0 and seq % KEY_CHUNK == 0 and seq % Q_TILE == 0

    cond = jnp.concatenate([c, c_ctx[None, :]], axis=0)
    cond_b = jnp.broadcast_to(cond[:, :, None], (batch + 1, d, LANES))
    mod = _adaln(cond_b, w_mod[0], b_mod[0][None, :])
    shift = mod[:batch + 1, None, 0 * d:1 * d]
    scale = mod[:batch + 1, None, 1 * d:2 * d]
    gate = mod[:batch, None, 2 * d:3 * d]

    g_in = norm_g[0][None, :]
    q_gain = (q_norm_g[0] * (math.log2(math.e) / math.sqrt(HEAD_DIM)))[None, :]
    k_gain = k_norm_g[0][None, :]

    ctx2d = ctx.reshape(batch * ctx_len, d)
    h_ctx = _prenorm(ctx2d, g_in, scale[batch:], shift[batch:], ctx_len, batch)
    no_rope = (jnp.ones((ctx_len, HEAD_DIM), _F32), jnp.zeros((ctx_len, HEAD_DIM), _F32),
               jnp.zeros((ctx_len, HEAD_DIM), _F32))
    k_ctx, vt_ctx = _proj_kv(h_ctx, w_in[0], k_gain, no_rope, batch, ctx_len, ctx_len)

    x2d = x.reshape(batch * seq, d)
    tables = _rope_tables(seq)
    h, q, k, vt, ga, w_conv_bf = _proj_attn(x2d, g_in, scale[:batch], shift[:batch], w_in[0],
                                            q_gain, k_gain, tables, batch, seq, ROW_TILE)
    u, e, w_out_bf = _proj_conv(h, w_conv_bf, w_out[0], ROW_TILE)

    attn = _attention(q, k, vt, k_ctx, vt_ctx, ga, batch, seq)
    out = _out_proj(attn, u, e, conv_w[0], w_out_bf, x2d, gate, final_norm_g[None, :], seq, ROW_TILE)
    return out.reshape(batch, seq, d)
```

```python
import functools
import math

import jax
import jax.numpy as jnp
import numpy as np
from jax import lax
from jax.experimental import pallas as pl
from jax.experimental.pallas import tpu as pltpu

D_MODEL = 2048
CTX_LEN = 256
GRID_W = 64
ATTN_WIDTH = 1024
CONV_WIDTH = 1024
HEAD_DIM = 128
N_HEADS = 8
N_KV_HEADS = 2
GROUP = N_HEADS // N_KV_HEADS
KV_WIDTH = N_KV_HEADS * HEAD_DIM
ROPE_THETA = 10000.0
EPS = 1e-6

LANES = 128
ROW_TILE = 512
COL_TILE = 512
KEY_CHUNK = 512
Q_TILE = 512
V_ROWS = HEAD_DIM + 16
HALO_ROWS = 16
MOD_ROWS = 8
MOD_COL_TILE = 768
VMEM_LIMIT = 56 * 1024 * 1024
assert 2 * KV_WIDTH == COL_TILE and ATTN_WIDTH % COL_TILE == 0 and CONV_WIDTH % COL_TILE == 0

_F32 = jnp.float32
_BF16 = jnp.bfloat16


def _silu(v):
    return v * (1.0 / (1.0 + jnp.exp(-v)))


def _params(n_axes, flags=None):
    return pltpu.CompilerParams(dimension_semantics=("arbitrary",) * n_axes,
                                vmem_limit_bytes=VMEM_LIMIT, flags=flags)


def _adaln_kernel(cb_ref, w_ref, b_ref, o_ref, s_scr):
    @pl.when(pl.program_id(0) == 0)
    def _():
        s_scr[...] = _silu(cb_ref[...])

    n_cond, n_chunks = s_scr.shape[0], o_ref.shape[1] // LANES

    def slab(kb, accs):
        rows = pl.ds(pl.multiple_of(kb * 8, 8), 8)
        w8 = w_ref[rows, :]
        out = []
        for r in range(n_cond):
            s8 = s_scr[r, rows, :]
            for c in range(n_chunks):
                out.append(accs[r * n_chunks + c] + w8[:, c * LANES:(c + 1) * LANES] * s8)
        return tuple(out)

    zeros = tuple(jnp.zeros((8, LANES), _F32) for _ in range(n_cond * n_chunks))
    accs = lax.fori_loop(0, w_ref.shape[0] // 8, slab, zeros, unroll=8)
    o_ref[...] = jnp.zeros(o_ref.shape, _F32)
    for r in range(n_cond):
        for c in range(n_chunks):
            cols = slice(c * LANES, (c + 1) * LANES)
            o_ref[r:r + 1, cols] = accs[r * n_chunks + c].sum(axis=0, keepdims=True) + b_ref[:, cols]


def _adaln(cond_b, w_mod, b_mod):
    d, n = w_mod.shape
    return pl.pallas_call(
        _adaln_kernel,
        grid=(n // MOD_COL_TILE,),
        in_specs=[pl.BlockSpec((3, d, LANES), lambda j: (0, 0, 0)),
                  pl.BlockSpec((d, MOD_COL_TILE), lambda j: (0, j)),
                  pl.BlockSpec((1, MOD_COL_TILE), lambda j: (0, j))],
        out_specs=pl.BlockSpec((MOD_ROWS, MOD_COL_TILE), lambda j: (0, j)),
        out_shape=jax.ShapeDtypeStruct((MOD_ROWS, n), _F32),
        scratch_shapes=[pltpu.VMEM((3, d, LANES), _F32)],
        compiler_params=_params(1),
        name="adaln",
    )(cond_b, w_mod, b_mod)


def _modulated_norm(x_ref, gin_ref, scale_ref, shift_ref):
    xf = x_ref[...]
    y = xf * lax.rsqrt(jnp.mean(xf * xf, axis=-1, keepdims=True) + EPS) * gin_ref[...]
    return (y * (1.0 + scale_ref[0]) + shift_ref[0]).astype(_BF16)


def _norm_rope(xh, g, rope):
    y = xh * lax.rsqrt(jnp.mean(xh * xh, axis=-1, keepdims=True) + EPS) * g
    if rope is None:
        return y
    cos, sin_lo, sin_hi = rope
    return (y * cos + pltpu.roll(y, 3 * HEAD_DIM // 4, axis=1) * sin_lo
            + pltpu.roll(y, HEAD_DIM // 4, axis=1) * sin_hi)


def _store_kv(acc, g, rope, k_ref, vt_ref):
    for hh in range(N_KV_HEADS):
        xh = acc[:, hh * HEAD_DIM:(hh + 1) * HEAD_DIM]
        k_ref[0, hh] = _norm_rope(xh, g, rope).astype(_BF16)
        vh = acc[:, KV_WIDTH + hh * HEAD_DIM:KV_WIDTH + (hh + 1) * HEAD_DIM]
        vt_ref[0, hh, 0:HEAD_DIM] = vh.T.astype(_BF16)
        vt_ref[0, hh, HEAD_DIM:V_ROWS] = jnp.ones((V_ROWS - HEAD_DIM, vh.shape[0]), _BF16)


def _ctx_kv_kernel(x_ref, gin_ref, scale_ref, shift_ref, w_ref, g_ref, k_ref, vt_ref):
    h = _modulated_norm(x_ref, gin_ref, scale_ref, shift_ref)
    acc = jnp.dot(h, w_ref[...].astype(_BF16), preferred_element_type=_F32)
    _store_kv(acc, g_ref[...], None, k_ref, vt_ref)


def _proj_attn_kernel(x_ref, gin_ref, scale_ref, shift_ref, w_ref, qg_ref, kg_ref,
                      cos_ref, slo_ref, shi_ref, wside_ref, h_ref, q_ref, k_ref, vt_ref, ga_ref,
                      wside_bf_ref):
    wside_bf_ref[...] = wside_ref[...].astype(_BF16)
    h = _modulated_norm(x_ref, gin_ref, scale_ref, shift_ref)
    h_ref[...] = h
    rope = (cos_ref[...], slo_ref[...], shi_ref[...])
    heads_per_dot = COL_TILE // HEAD_DIM

    def project(col0):
        w = w_ref[:, col0:col0 + COL_TILE].astype(_BF16)
        return jnp.dot(h, w, preferred_element_type=_F32)

    for blk in range(ATTN_WIDTH // COL_TILE):
        acc = project(blk * COL_TILE)
        for hh in range(heads_per_dot):
            xh = acc[:, hh * HEAD_DIM:(hh + 1) * HEAD_DIM]
            q_ref[0, blk * heads_per_dot + hh] = _norm_rope(xh, qg_ref[...], rope).astype(_BF16)
    _store_kv(project(ATTN_WIDTH), kg_ref[...], rope, k_ref, vt_ref)
    gate_col = ATTN_WIDTH + 2 * KV_WIDTH
    for blk in range(ATTN_WIDTH // COL_TILE):
        acc = project(gate_col + blk * COL_TILE)
        ga_ref[:, blk * COL_TILE:(blk + 1) * COL_TILE] = _silu(acc).astype(_BF16)


def _proj_conv_kernel(h_ref, wb_ref, wcg_ref, wh_ref, wgc_ref, wside_ref, u_ref, e_ref,
                      wside_bf_ref):
    wside_bf_ref[...] = wside_ref[...].astype(_BF16)
    h = h_ref[...]
    cg = jnp.dot(h, wcg_ref[...], preferred_element_type=_F32)
    hh = jnp.dot(h, wh_ref[...], preferred_element_type=_F32)
    u_ref[...] = (cg * hh).astype(_BF16)
    b = jnp.dot(h, wb_ref[...], preferred_element_type=_F32)
    gc = jnp.dot(h, wgc_ref[...], preferred_element_type=_F32)
    e_ref[...] = (b * _silu(gc)).astype(_BF16)


def _rope_specs(tm, seq):
    tiles = seq // tm
    return [pl.BlockSpec((tm, HEAD_DIM), lambda i, *_: (i % tiles, 0))] * 3


def _proj_attn(x2d, gin, scale, shift, w_in, q_gain, k_gain, tables, batch, seq, tm):
    m, d = x2d.shape
    tiles = seq // tm
    steps = m // tm
    n_cols = 2 * ATTN_WIDTH + 2 * KV_WIDTH
    side_cols = (w_in.shape[1] - n_cols) // steps
    assert side_cols % LANES == 0 and n_cols % side_cols == 0
    side_first = n_cols // side_cols
    mod_spec = pl.BlockSpec((1, 1, d), lambda i: (i // tiles, 0, 0))
    gain_spec = pl.BlockSpec((1, HEAD_DIM), lambda i: (0, 0))
    return pl.pallas_call(
        _proj_attn_kernel,
        grid=(steps,),
        in_specs=[pl.BlockSpec((tm, d), lambda i: (i, 0)),
                  pl.BlockSpec((1, d), lambda i: (0, 0)),
                  mod_spec, mod_spec,
                  pl.BlockSpec((d, n_cols), lambda i: (0, 0)),
                  gain_spec, gain_spec] + _rope_specs(tm, seq)
                 + [pl.BlockSpec((d, side_cols), lambda i: (0, side_first + i))],
        out_specs=[pl.BlockSpec((tm, d), lambda i: (i, 0)),
                   pl.BlockSpec((1, N_HEADS, tm, HEAD_DIM), lambda i: (i // tiles, 0, i % tiles, 0)),
                   pl.BlockSpec((1, N_KV_HEADS, tm, HEAD_DIM), lambda i: (i // tiles, 0, i % tiles, 0)),
                   pl.BlockSpec((1, N_KV_HEADS, V_ROWS, tm), lambda i: (i // tiles, 0, 0, i % tiles)),
                   pl.BlockSpec((tm, ATTN_WIDTH), lambda i: (i, 0)),
                   pl.BlockSpec((d, side_cols), lambda i: (0, i))],
        out_shape=[jax.ShapeDtypeStruct((m, d), _BF16),
                   jax.ShapeDtypeStruct((batch, N_HEADS, seq, HEAD_DIM), _BF16),
                   jax.ShapeDtypeStruct((batch, N_KV_HEADS, seq, HEAD_DIM), _BF16),
                   jax.ShapeDtypeStruct((batch, N_KV_HEADS, V_ROWS, seq), _BF16),
                   jax.ShapeDtypeStruct((m, ATTN_WIDTH), _BF16),
                   jax.ShapeDtypeStruct((d, side_cols * steps), _BF16)],
        compiler_params=_params(1),
        name="proj_attn",
    )(x2d, gin, scale, shift, w_in, q_gain, k_gain, *tables, w_in)


def _ctx_kv(ctx2d, gin, scale, shift, w_in, k_gain, batch, ctx_len):
    d = ctx2d.shape[1]
    kv_col_block = ATTN_WIDTH // COL_TILE
    mod_spec = pl.BlockSpec((1, 1, d), lambda i: (0, 0, 0))
    return pl.pallas_call(
        _ctx_kv_kernel,
        grid=(batch,),
        in_specs=[pl.BlockSpec((ctx_len, d), lambda i: (i, 0)),
                  pl.BlockSpec((1, d), lambda i: (0, 0)),
                  mod_spec, mod_spec,
                  pl.BlockSpec((d, 2 * KV_WIDTH), lambda i: (0, kv_col_block)),
                  pl.BlockSpec((1, HEAD_DIM), lambda i: (0, 0))],
        out_specs=[pl.BlockSpec((1, N_KV_HEADS, ctx_len, HEAD_DIM), lambda i: (i, 0, 0, 0)),
                   pl.BlockSpec((1, N_KV_HEADS, V_ROWS, ctx_len), lambda i: (i, 0, 0, 0))],
        out_shape=[jax.ShapeDtypeStruct((batch, N_KV_HEADS, ctx_len, HEAD_DIM), _BF16),
                   jax.ShapeDtypeStruct((batch, N_KV_HEADS, V_ROWS, ctx_len), _BF16)],
        compiler_params=_params(1),
        name="ctx_kv",
    )(ctx2d, gin, scale, shift, w_in, k_gain)


def _proj_conv(h, w_conv_bf, w_out, tm):
    m, d = h.shape
    step = CONV_WIDTH // COL_TILE
    side_rows = w_out.shape[0] // (m // tm)
    assert side_rows % HALO_ROWS == 0

    def w_spec(group):
        return pl.BlockSpec((d, COL_TILE), lambda j, i: (0, group * step + j))

    out_spec = pl.BlockSpec((tm, COL_TILE), lambda j, i: (i, j))
    side_spec = pl.BlockSpec((side_rows, w_out.shape[1]), lambda j, i: (i, 0))
    return pl.pallas_call(
        _proj_conv_kernel,
        grid=(step, m // tm),
        in_specs=[pl.BlockSpec((tm, d), lambda j, i: (i, 0)),
                  w_spec(0), w_spec(1), w_spec(2), w_spec(3), side_spec],
        out_specs=[out_spec, out_spec, side_spec],
        out_shape=[jax.ShapeDtypeStruct((m, CONV_WIDTH), _BF16)] * 2
                  + [jax.ShapeDtypeStruct(w_out.shape, _BF16)],
        compiler_params=_params(2),
        name="proj_conv",
    )(h, w_conv_bf, w_conv_bf, w_conv_bf, w_conv_bf, w_out)


def _attn_kernel(q_ref, k_ref, vt_ref, kc_ref, vtc_ref, ga_ref, o_ref, s_ring, m_ring, p_ring):
    n_chunks, kc = k_ref.shape[2], k_ref.shape[3]
    n_lat, ctx = n_chunks * kc, kc_ref.shape[2]
    tq = s_ring[0].shape[1]
    n_tiles = q_ref.shape[2] // tq
    nt = (((1,), (1,)), ((), ()))

    def fold(v):
        return v.reshape(v.shape[0] // 8, 8, v.shape[1])

    def rows(t):
        return pl.ds(pl.multiple_of(t * tq, tq), tq)

    def key_block(c):
        return (k_ref[0, 0, c], pl.ds(c * kc, kc)) if c < n_chunks else (kc_ref[0, 0], pl.ds(n_lat, ctx))

    def scores(t, g, c):
        keys, span = key_block(c)
        s = lax.dot_general(keys, q_ref[0, g, rows(t), :], nt, preferred_element_type=_F32)
        s_ring[g % 2][span, :] = s
        chunk_max = fold(s).max(axis=0)
        m_buf = m_ring[g % 2]
        m_buf[...] = chunk_max if c == 0 else jnp.maximum(m_buf[...], chunk_max)

    def probs(g, c, m):
        _, span = key_block(c)
        p_ring[g % 2][span, :] = jnp.exp2(s_ring[g % 2][span, :] - m).astype(_BF16)

    def values(t, g):
        p_buf = p_ring[g % 2]
        acc = jnp.dot(vt_ref[0, 0], p_buf[0:n_lat, :], preferred_element_type=_F32)
        acc += jnp.dot(vtc_ref[0, 0], p_buf[n_lat:n_lat + ctx, :], preferred_element_type=_F32)
        o = (acc[0:HEAD_DIM] * (1.0 / acc[HEAD_DIM:HEAD_DIM + 1])).T
        cols = slice(g * HEAD_DIM, (g + 1) * HEAD_DIM)
        o_ref[rows(t), cols] = (o * ga_ref[rows(t), cols].astype(_F32)).astype(_BF16)

    def phase(score_item, prob_head, value_item):
        if prob_head is not None:
            m = m_ring[prob_head % 2][...].max(axis=0, keepdims=True)
        for c in range(n_chunks + 1):
            if score_item is not None:
                scores(*score_item, c)
            if prob_head is not None:
                probs(prob_head, c, m)
        if value_item is not None:
            values(*value_item)

    assert GROUP == 4
    phase((0, 0), None, None)
    phase((0, 1), 0, None)

    def tile(t, carry):
        nxt = jnp.minimum(t + 1, n_tiles - 1)
        phase((t, 2), 1, (t, 0))
        phase((t, 3), 2, (t, 1))
        phase((nxt, 0), 3, (t, 2))
        phase((nxt, 1), 0, (t, 3))
        return carry

    lax.fori_loop(0, n_tiles, tile, 0)


def _attention(q, k, vt, k_ctx, vt_ctx, ga, batch, seq):
    n_chunks = seq // KEY_CHUNK
    ctx = k_ctx.shape[2]
    k5 = k.reshape(batch, N_KV_HEADS, n_chunks, KEY_CHUNK, HEAD_DIM)
    group_w = GROUP * HEAD_DIM
    n_keys = seq + ctx
    return pl.pallas_call(
        _attn_kernel,
        grid=(batch, N_KV_HEADS),
        in_specs=[pl.BlockSpec((1, GROUP, seq, HEAD_DIM), lambda b, kv: (b, kv, 0, 0),
                               pipeline_mode=pl.Buffered(1)),
                  pl.BlockSpec((1, 1, n_chunks, KEY_CHUNK, HEAD_DIM), lambda b, kv: (b, kv, 0, 0, 0)),
                  pl.BlockSpec((1, 1, V_ROWS, seq), lambda b, kv: (b, kv, 0, 0)),
                  pl.BlockSpec((1, 1, ctx, HEAD_DIM), lambda b, kv: (b, kv, 0, 0)),
                  pl.BlockSpec((1, 1, V_ROWS, ctx), lambda b, kv: (b, kv, 0, 0)),
                  pl.BlockSpec((seq, group_w), lambda b, kv: (b, kv), pipeline_mode=pl.Buffered(1))],
        out_specs=pl.BlockSpec((seq, group_w), lambda b, kv: (b, kv)),
        out_shape=jax.ShapeDtypeStruct((batch * seq, ATTN_WIDTH), _BF16),
        scratch_shapes=[[pltpu.VMEM((n_keys, Q_TILE), _F32)] * 2,
                        [pltpu.VMEM((8, Q_TILE), _F32)] * 2,
                        [pltpu.VMEM((n_keys, Q_TILE), _BF16)] * 2],
        compiler_params=_params(2),
        name="attention",
    )(q, k5, vt, k_ctx, vt_ctx, ga)


def _out_kernel(tiles_per_seq, attn_ref, u_ref, e_ref, up_ref, un_ref, cw_ref, wa_ref, wc_ref,
                x_ref, gate_ref, gf_ref, o_ref):
    i = pl.program_id(0)
    tm = u_ref.shape[0]
    u = u_ref[...].astype(_F32)
    first = (i % tiles_per_seq) == 0
    last = (i % tiles_per_seq) == tiles_per_seq - 1
    prev_row = jnp.where(first, 0.0, up_ref[HALO_ROWS - 1:HALO_ROWS, :].astype(_F32))
    next_row = jnp.where(last, 0.0, un_ref[0:1, :].astype(_F32))
    slab_row = lax.broadcasted_iota(jnp.int32, (8, 1), 0)
    down, up = pltpu.roll(u, 1, axis=0), pltpu.roll(u, tm - 1, axis=0)
    u_before = jnp.concatenate([jnp.where(slab_row == 0, prev_row, down[0:8]), down[8:]], axis=0)
    u_after = jnp.concatenate([up[:tm - 8], jnp.where(slab_row == 7, next_row, up[tm - 8:])], axis=0)
    conv = u_before * cw_ref[0:1, :] + u * cw_ref[1:2, :] + u_after * cw_ref[2:3, :]
    cv = (e_ref[...].astype(_F32) * conv).astype(_BF16)
    y = jnp.dot(attn_ref[...], wa_ref[...], preferred_element_type=_F32)
    y += jnp.dot(cv, wc_ref[...], preferred_element_type=_F32)
    xn = x_ref[...] + gate_ref[0] * y
    o_ref[...] = xn * lax.rsqrt(jnp.mean(xn * xn, axis=-1, keepdims=True) + EPS) * gf_ref[...]


def _out_proj(attn, u, e, conv_w, w_out_bf, x2d, gate, gf, seq, tm):
    m, d = x2d.shape
    tiles_per_seq = seq // tm
    halo_per_tile = tm // HALO_ROWS
    n_halo = m // HALO_ROWS
    mix_spec = pl.BlockSpec((tm, ATTN_WIDTH), lambda i: (i, 0))
    return pl.pallas_call(
        functools.partial(_out_kernel, tiles_per_seq),
        grid=(m // tm,),
        in_specs=[mix_spec, mix_spec, mix_spec,
                  pl.BlockSpec((HALO_ROWS, CONV_WIDTH),
                               lambda i: (jnp.maximum(i * halo_per_tile - 1, 0), 0)),
                  pl.BlockSpec((HALO_ROWS, CONV_WIDTH),
                               lambda i: (jnp.minimum((i + 1) * halo_per_tile, n_halo - 1), 0)),
                  pl.BlockSpec((3, CONV_WIDTH), lambda i: (0, 0)),
                  pl.BlockSpec((ATTN_WIDTH, d), lambda i: (0, 0)),
                  pl.BlockSpec((CONV_WIDTH, d), lambda i: (1, 0)),
                  pl.BlockSpec((tm, d), lambda i: (i, 0)),
                  pl.BlockSpec((1, 1, d), lambda i: (i // tiles_per_seq, 0, 0)),
                  pl.BlockSpec((1, d), lambda i: (0, 0))],
        out_specs=pl.BlockSpec((tm, d), lambda i: (i, 0)),
        out_shape=jax.ShapeDtypeStruct((m, d), _F32),
        compiler_params=_params(1),
        name="out_proj",
    )(attn, u, e, u, u, conv_w, w_out_bf, w_out_bf, x2d, gate, gf)


def _rope_tables(seq):
    quarter = HEAD_DIM // 4
    t = np.arange(seq)
    row = (t // GRID_W).astype(np.float64)
    col = (t % GRID_W).astype(np.float64)
    inv = ROPE_THETA ** (-np.arange(quarter, dtype=np.float64) / quarter)
    ang = np.concatenate([row[:, None] * inv, row[:, None] * inv,
                          col[:, None] * inv, col[:, None] * inv], axis=1)
    lane = np.arange(HEAD_DIM) % (2 * quarter)
    sin = np.sin(ang)
    tables = (np.cos(ang), np.where(lane < quarter, -sin, 0.0), np.where(lane >= quarter, sin, 0.0))
    return tuple(jnp.asarray(tab.astype(np.float32)) for tab in tables)


def kernel(x, c, ctx, c_ctx, w_mod, b_mod, norm_g, w_in, q_norm_g, k_norm_g, conv_w, w_out,
           final_norm_g):
    batch, seq, d = x.shape
    ctx_len = ctx.shape[1]
    assert w_mod.shape[0] == 1, "single-layer kernel"
    assert seq % ROW_TILE == 0 and seq % KEY_CHUNK == 0 and seq % Q_TILE == 0

    cond = jnp.concatenate([c, c_ctx[None, :]], axis=0)
    cond_b = jnp.broadcast_to(cond[:, :, None], (batch + 1, d, LANES))
    mod = _adaln(cond_b, w_mod[0], b_mod[0][None, :])
    shift = mod[:batch + 1, None, 0 * d:1 * d]
    scale = mod[:batch + 1, None, 1 * d:2 * d]
    gate = mod[:batch, None, 2 * d:3 * d]

    g_in = norm_g[0][None, :]
    q_gain = (q_norm_g[0] * (math.log2(math.e) / math.sqrt(HEAD_DIM)))[None, :]
    k_gain = k_norm_g[0][None, :]

    ctx2d = ctx.reshape(batch * ctx_len, d)
    k_ctx, vt_ctx = _ctx_kv(ctx2d, g_in, scale[batch:], shift[batch:], w_in[0], k_gain, batch, ctx_len)

    x2d = x.reshape(batch * seq, d)
    tables = _rope_tables(seq)
    h, q, k, vt, ga, w_conv_bf = _proj_attn(x2d, g_in, scale[:batch], shift[:batch], w_in[0],
                                            q_gain, k_gain, tables, batch, seq, ROW_TILE)
    u, e, w_out_bf = _proj_conv(h, w_conv_bf, w_out[0], ROW_TILE)

    attn = _attention(q, k, vt, k_ctx, vt_ctx, ga, batch, seq)
    out = _out_proj(attn, u, e, conv_w[0], w_out_bf, x2d, gate, final_norm_g[None, :], seq, ROW_TILE)
    return out.reshape(batch, seq, d)
```

```python
import functools
import math

import jax
import jax.numpy as jnp
import numpy as np
from jax import lax
from jax.experimental import pallas as pl
from jax.experimental.pallas import tpu as pltpu

D_MODEL = 2048
CTX_LEN = 256
GRID_W = 64
ATTN_WIDTH = 1024
CONV_WIDTH = 1024
HEAD_DIM = 128
N_HEADS = 8
N_KV_HEADS = 2
GROUP = N_HEADS // N_KV_HEADS
KV_WIDTH = N_KV_HEADS * HEAD_DIM
ROPE_THETA = 10000.0
EPS = 1e-6

LANES = 128
ROW_TILE = 512
CONV_ROW_TILE = 1024
SUB_ROWS = 512
COL_TILE = 512
KEY_CHUNK = 512
Q_TILE = 256
V_ROWS = HEAD_DIM + 16
HALO_ROWS = 16
MOD_ROWS = 8
MOD_COL_TILE = 768
VMEM_LIMIT = 56 * 1024 * 1024
assert 2 * KV_WIDTH == COL_TILE and ATTN_WIDTH % COL_TILE == 0 and CONV_WIDTH % COL_TILE == 0

_F32 = jnp.float32
_BF16 = jnp.bfloat16


def _silu(v):
    return v * (1.0 / (1.0 + jnp.exp(-v)))


def _params(n_axes, flags=None):
    return pltpu.CompilerParams(dimension_semantics=("arbitrary",) * n_axes,
                                vmem_limit_bytes=VMEM_LIMIT, flags=flags)


def _adaln_kernel(cb_ref, w_ref, b_ref, o_ref, s_scr):
    @pl.when(pl.program_id(0) == 0)
    def _():
        s_scr[...] = _silu(cb_ref[...])

    n_cond, n_chunks = s_scr.shape[0], o_ref.shape[1] // LANES

    def slab(kb, accs):
        rows = pl.ds(pl.multiple_of(kb * 8, 8), 8)
        w8 = w_ref[rows, :]
        out = []
        for r in range(n_cond):
            s8 = s_scr[r, rows, :]
            for c in range(n_chunks):
                out.append(accs[r * n_chunks + c] + w8[:, c * LANES:(c + 1) * LANES] * s8)
        return tuple(out)

    zeros = tuple(jnp.zeros((8, LANES), _F32) for _ in range(n_cond * n_chunks))
    accs = lax.fori_loop(0, w_ref.shape[0] // 8, slab, zeros, unroll=8)
    o_ref[...] = jnp.zeros(o_ref.shape, _F32)
    for r in range(n_cond):
        for c in range(n_chunks):
            cols = slice(c * LANES, (c + 1) * LANES)
            o_ref[r:r + 1, cols] = accs[r * n_chunks + c].sum(axis=0, keepdims=True) + b_ref[:, cols]


def _adaln(cond_b, w_mod, b_mod):
    d, n = w_mod.shape
    return pl.pallas_call(
        _adaln_kernel,
        grid=(n // MOD_COL_TILE,),
        in_specs=[pl.BlockSpec((3, d, LANES), lambda j: (0, 0, 0)),
                  pl.BlockSpec((d, MOD_COL_TILE), lambda j: (0, j)),
                  pl.BlockSpec((1, MOD_COL_TILE), lambda j: (0, j))],
        out_specs=pl.BlockSpec((MOD_ROWS, MOD_COL_TILE), lambda j: (0, j)),
        out_shape=jax.ShapeDtypeStruct((MOD_ROWS, n), _F32),
        scratch_shapes=[pltpu.VMEM((3, d, LANES), _F32)],
        compiler_params=_params(1),
        name="adaln",
    )(cond_b, w_mod, b_mod)


def _modulated_norm(x_ref, gin_ref, scale_ref, shift_ref):
    xf = x_ref[...]
    y = xf * lax.rsqrt(jnp.mean(xf * xf, axis=-1, keepdims=True) + EPS) * gin_ref[...]
    return (y * (1.0 + scale_ref[0]) + shift_ref[0]).astype(_BF16)


def _norm_rope(xh, g, rope):
    y = xh * lax.rsqrt(jnp.mean(xh * xh, axis=-1, keepdims=True) + EPS) * g
    if rope is None:
        return y
    cos, sin_lo, sin_hi = rope
    return (y * cos + pltpu.roll(y, 3 * HEAD_DIM // 4, axis=1) * sin_lo
            + pltpu.roll(y, HEAD_DIM // 4, axis=1) * sin_hi)


def _store_kv(acc, g, rope, k_ref, vt_ref):
    for hh in range(N_KV_HEADS):
        xh = acc[:, hh * HEAD_DIM:(hh + 1) * HEAD_DIM]
        k_ref[0, hh] = _norm_rope(xh, g, rope).astype(_BF16)
        vh = acc[:, KV_WIDTH + hh * HEAD_DIM:KV_WIDTH + (hh + 1) * HEAD_DIM]
        vt_ref[0, hh, 0:HEAD_DIM] = vh.T.astype(_BF16)
        vt_ref[0, hh, HEAD_DIM:V_ROWS] = jnp.ones((V_ROWS - HEAD_DIM, vh.shape[0]), _BF16)


def _ctx_kv_kernel(x_ref, gin_ref, scale_ref, shift_ref, w_ref, g_ref, k_ref, vt_ref):
    h = _modulated_norm(x_ref, gin_ref, scale_ref, shift_ref)
    acc = jnp.dot(h, w_ref[...].astype(_BF16), preferred_element_type=_F32)
    _store_kv(acc, g_ref[...], None, k_ref, vt_ref)


def _proj_attn_kernel(x_ref, gin_ref, scale_ref, shift_ref, w_ref, qg_ref, kg_ref,
                      cos_ref, slo_ref, shi_ref, wside_ref, h_ref, q_ref, k_ref, vt_ref, ga_ref,
                      wside_bf_ref):
    wside_bf_ref[...] = wside_ref[...].astype(_BF16)
    h = _modulated_norm(x_ref, gin_ref, scale_ref, shift_ref)
    h_ref[...] = h
    rope = (cos_ref[...], slo_ref[...], shi_ref[...])
    heads_per_dot = COL_TILE // HEAD_DIM

    def project(col0):
        w = w_ref[:, col0:col0 + COL_TILE].astype(_BF16)
        return jnp.dot(h, w, preferred_element_type=_F32)

    for blk in range(ATTN_WIDTH // COL_TILE):
        acc = project(blk * COL_TILE)
        for hh in range(heads_per_dot):
            xh = acc[:, hh * HEAD_DIM:(hh + 1) * HEAD_DIM]
            q_ref[0, blk * heads_per_dot + hh] = _norm_rope(xh, qg_ref[...], rope).astype(_BF16)
    _store_kv(project(ATTN_WIDTH), kg_ref[...], rope, k_ref, vt_ref)
    gate_col = ATTN_WIDTH + 2 * KV_WIDTH
    for blk in range(ATTN_WIDTH // COL_TILE):
        acc = project(gate_col + blk * COL_TILE)
        ga_ref[:, blk * COL_TILE:(blk + 1) * COL_TILE] = _silu(acc).astype(_BF16)


def _proj_conv_kernel(h_ref, wb_ref, wcg_ref, wh_ref, wgc_ref, wside_ref, u_ref, e_ref,
                      wside_bf_ref):
    wside_bf_ref[...] = wside_ref[...].astype(_BF16)
    for r0 in range(0, h_ref.shape[0], SUB_ROWS):
        rows = slice(r0, r0 + SUB_ROWS)
        h = h_ref[rows, :]
        cg = jnp.dot(h, wcg_ref[...], preferred_element_type=_F32)
        hh = jnp.dot(h, wh_ref[...], preferred_element_type=_F32)
        u_ref[rows, :] = (cg * hh).astype(_BF16)
        b = jnp.dot(h, wb_ref[...], preferred_element_type=_F32)
        gc = jnp.dot(h, wgc_ref[...], preferred_element_type=_F32)
        e_ref[rows, :] = (b * _silu(gc)).astype(_BF16)


def _rope_specs(tm, seq):
    tiles = seq // tm
    return [pl.BlockSpec((tm, HEAD_DIM), lambda i, *_: (i % tiles, 0))] * 3


def _proj_attn(x2d, gin, scale, shift, w_in, q_gain, k_gain, tables, batch, seq, tm):
    m, d = x2d.shape
    tiles = seq // tm
    steps = m // tm
    n_cols = 2 * ATTN_WIDTH + 2 * KV_WIDTH
    side_cols = (w_in.shape[1] - n_cols) // steps
    assert side_cols % LANES == 0 and n_cols % side_cols == 0
    side_first = n_cols // side_cols
    mod_spec = pl.BlockSpec((1, 1, d), lambda i: (i // tiles, 0, 0))
    gain_spec = pl.BlockSpec((1, HEAD_DIM), lambda i: (0, 0))
    return pl.pallas_call(
        _proj_attn_kernel,
        grid=(steps,),
        in_specs=[pl.BlockSpec((tm, d), lambda i: (i, 0)),
                  pl.BlockSpec((1, d), lambda i: (0, 0)),
                  mod_spec, mod_spec,
                  pl.BlockSpec((d, n_cols), lambda i: (0, 0)),
                  gain_spec, gain_spec] + _rope_specs(tm, seq)
                 + [pl.BlockSpec((d, side_cols), lambda i: (0, side_first + i))],
        out_specs=[pl.BlockSpec((tm, d), lambda i: (i, 0)),
                   pl.BlockSpec((1, N_HEADS, tm, HEAD_DIM), lambda i: (i // tiles, 0, i % tiles, 0)),
                   pl.BlockSpec((1, N_KV_HEADS, tm, HEAD_DIM), lambda i: (i // tiles, 0, i % tiles, 0)),
                   pl.BlockSpec((1, N_KV_HEADS, V_ROWS, tm), lambda i: (i // tiles, 0, 0, i % tiles)),
                   pl.BlockSpec((tm, ATTN_WIDTH), lambda i: (i, 0)),
                   pl.BlockSpec((d, side_cols), lambda i: (0, i))],
        out_shape=[jax.ShapeDtypeStruct((m, d), _BF16),
                   jax.ShapeDtypeStruct((batch, N_HEADS, seq, HEAD_DIM), _BF16),
                   jax.ShapeDtypeStruct((batch, N_KV_HEADS, seq, HEAD_DIM), _BF16),
                   jax.ShapeDtypeStruct((batch, N_KV_HEADS, V_ROWS, seq), _BF16),
                   jax.ShapeDtypeStruct((m, ATTN_WIDTH), _BF16),
                   jax.ShapeDtypeStruct((d, side_cols * steps), _BF16)],
        compiler_params=_params(1),
        name="proj_attn",
    )(x2d, gin, scale, shift, w_in, q_gain, k_gain, *tables, w_in)


def _ctx_kv(ctx2d, gin, scale, shift, w_in, k_gain, batch, ctx_len):
    d = ctx2d.shape[1]
    kv_col_block = ATTN_WIDTH // COL_TILE
    mod_spec = pl.BlockSpec((1, 1, d), lambda i: (0, 0, 0))
    return pl.pallas_call(
        _ctx_kv_kernel,
        grid=(batch,),
        in_specs=[pl.BlockSpec((ctx_len, d), lambda i: (i, 0)),
                  pl.BlockSpec((1, d), lambda i: (0, 0)),
                  mod_spec, mod_spec,
                  pl.BlockSpec((d, 2 * KV_WIDTH), lambda i: (0, kv_col_block)),
                  pl.BlockSpec((1, HEAD_DIM), lambda i: (0, 0))],
        out_specs=[pl.BlockSpec((1, N_KV_HEADS, ctx_len, HEAD_DIM), lambda i: (i, 0, 0, 0)),
                   pl.BlockSpec((1, N_KV_HEADS, V_ROWS, ctx_len), lambda i: (i, 0, 0, 0))],
        out_shape=[jax.ShapeDtypeStruct((batch, N_KV_HEADS, ctx_len, HEAD_DIM), _BF16),
                   jax.ShapeDtypeStruct((batch, N_KV_HEADS, V_ROWS, ctx_len), _BF16)],
        compiler_params=_params(1),
        name="ctx_kv",
    )(ctx2d, gin, scale, shift, w_in, k_gain)


def _proj_conv(h, w_conv_bf, w_out, tm):
    m, d = h.shape
    step = CONV_WIDTH // COL_TILE
    row_tiles = m // tm
    side_rows = w_out.shape[0] // (step * row_tiles)
    assert side_rows % HALO_ROWS == 0

    def w_spec(group):
        return pl.BlockSpec((d, COL_TILE), lambda j, i: (0, group * step + j))

    out_spec = pl.BlockSpec((tm, COL_TILE), lambda j, i: (i, j))
    side_spec = pl.BlockSpec((side_rows, w_out.shape[1]), lambda j, i: (j * row_tiles + i, 0))
    return pl.pallas_call(
        _proj_conv_kernel,
        grid=(step, m // tm),
        in_specs=[pl.BlockSpec((tm, d), lambda j, i: (i, 0)),
                  w_spec(0), w_spec(1), w_spec(2), w_spec(3), side_spec],
        out_specs=[out_spec, out_spec, side_spec],
        out_shape=[jax.ShapeDtypeStruct((m, CONV_WIDTH), _BF16)] * 2
                  + [jax.ShapeDtypeStruct(w_out.shape, _BF16)],
        compiler_params=_params(2),
        name="proj_conv",
    )(h, w_conv_bf, w_conv_bf, w_conv_bf, w_conv_bf, w_out)


def _attn_kernel(q_ref, k_ref, vt_ref, kc_ref, vtc_ref, ga_ref, o_ref, s_ring, m_ring, p_ring):
    n_chunks, kc = k_ref.shape[2], k_ref.shape[3]
    n_lat, ctx = n_chunks * kc, kc_ref.shape[2]
    tq = s_ring[0].shape[1]
    n_tiles = q_ref.shape[2] // tq
    nt = (((1,), (1,)), ((), ()))

    def fold(v):
        return v.reshape(v.shape[0] // 8, 8, v.shape[1])

    def rows(t):
        return pl.ds(pl.multiple_of(t * tq, tq), tq)

    def key_block(c):
        return (k_ref[0, 0, c], pl.ds(c * kc, kc)) if c < n_chunks else (kc_ref[0, 0], pl.ds(n_lat, ctx))

    def scores(t, g, c):
        keys, span = key_block(c)
        s = lax.dot_general(keys, q_ref[0, g, rows(t), :], nt, preferred_element_type=_F32)
        s_ring[g % 2][span, :] = s
        chunk_max = fold(s).max(axis=0)
        m_buf = m_ring[g % 2]
        m_buf[...] = chunk_max if c == 0 else jnp.maximum(m_buf[...], chunk_max)

    def probs(g, c, m):
        _, span = key_block(c)
        p_ring[g % 2][span, :] = jnp.exp2(s_ring[g % 2][span, :] - m).astype(_BF16)

    def values(t, g):
        p_buf = p_ring[g % 2]
        acc = jnp.dot(vt_ref[0, 0], p_buf[0:n_lat, :], preferred_element_type=_F32)
        acc += jnp.dot(vtc_ref[0, 0], p_buf[n_lat:n_lat + ctx, :], preferred_element_type=_F32)
        o = (acc[0:HEAD_DIM] * (1.0 / acc[HEAD_DIM:HEAD_DIM + 1])).T
        cols = slice(g * HEAD_DIM, (g + 1) * HEAD_DIM)
        o_ref[rows(t), cols] = (o * ga_ref[rows(t), cols].astype(_F32)).astype(_BF16)

    def phase(score_item, prob_head, value_item):
        if prob_head is not None:
            m = m_ring[prob_head % 2][...].max(axis=0, keepdims=True)
        for c in range(n_chunks + 1):
            if score_item is not None:
                scores(*score_item, c)
            if prob_head is not None:
                probs(prob_head, c, m)
        if value_item is not None:
            values(*value_item)

    assert GROUP == 4
    phase((0, 0), None, None)
    phase((0, 1), 0, None)

    def tile(t, carry):
        nxt = jnp.minimum(t + 1, n_tiles - 1)
        phase((t, 2), 1, (t, 0))
        phase((t, 3), 2, (t, 1))
        phase((nxt, 0), 3, (t, 2))
        phase((nxt, 1), 0, (t, 3))
        return carry

    lax.fori_loop(0, n_tiles, tile, 0)


def _attention(q, k, vt, k_ctx, vt_ctx, ga, batch, seq):
    n_chunks = seq // KEY_CHUNK
    ctx = k_ctx.shape[2]
    k5 = k.reshape(batch, N_KV_HEADS, n_chunks, KEY_CHUNK, HEAD_DIM)
    group_w = GROUP * HEAD_DIM
    n_keys = seq + ctx
    return pl.pallas_call(
        _attn_kernel,
        grid=(batch, N_KV_HEADS),
        in_specs=[pl.BlockSpec((1, GROUP, seq, HEAD_DIM), lambda b, kv: (b, kv, 0, 0)),
                  pl.BlockSpec((1, 1, n_chunks, KEY_CHUNK, HEAD_DIM), lambda b, kv: (b, kv, 0, 0, 0)),
                  pl.BlockSpec((1, 1, V_ROWS, seq), lambda b, kv: (b, kv, 0, 0)),
                  pl.BlockSpec((1, 1, ctx, HEAD_DIM), lambda b, kv: (b, kv, 0, 0)),
                  pl.BlockSpec((1, 1, V_ROWS, ctx), lambda b, kv: (b, kv, 0, 0)),
                  pl.BlockSpec((seq, group_w), lambda b, kv: (b, kv))],
        out_specs=pl.BlockSpec((seq, group_w), lambda b, kv: (b, kv)),
        out_shape=jax.ShapeDtypeStruct((batch * seq, ATTN_WIDTH), _BF16),
        scratch_shapes=[[pltpu.VMEM((n_keys, Q_TILE), _F32)] * 2,
                        [pltpu.VMEM((8, Q_TILE), _F32)] * 2,
                        [pltpu.VMEM((n_keys, Q_TILE), _BF16)] * 2],
        compiler_params=_params(2),
        name="attention",
    )(q, k5, vt, k_ctx, vt_ctx, ga)


def _out_kernel(tiles_per_seq, attn_ref, u_ref, e_ref, up_ref, un_ref, cw_ref, wa_ref, wc_ref,
                x_ref, gate_ref, gf_ref, o_ref):
    i = pl.program_id(0)
    tm = u_ref.shape[0]
    u = u_ref[...].astype(_F32)
    first = (i % tiles_per_seq) == 0
    last = (i % tiles_per_seq) == tiles_per_seq - 1
    prev_row = jnp.where(first, 0.0, up_ref[HALO_ROWS - 1:HALO_ROWS, :].astype(_F32))
    next_row = jnp.where(last, 0.0, un_ref[0:1, :].astype(_F32))
    slab_row = lax.broadcasted_iota(jnp.int32, (8, 1), 0)
    down, up = pltpu.roll(u, 1, axis=0), pltpu.roll(u, tm - 1, axis=0)
    u_before = jnp.concatenate([jnp.where(slab_row == 0, prev_row, down[0:8]), down[8:]], axis=0)
    u_after = jnp.concatenate([up[:tm - 8], jnp.where(slab_row == 7, next_row, up[tm - 8:])], axis=0)
    conv = u_before * cw_ref[0:1, :] + u * cw_ref[1:2, :] + u_after * cw_ref[2:3, :]
    cv = (e_ref[...].astype(_F32) * conv).astype(_BF16)
    half = tm // 2
    for r0 in range(0, tm, half):
        rows = slice(r0, r0 + half)
        y = jnp.dot(attn_ref[rows, :], wa_ref[...], preferred_element_type=_F32)
        y += jnp.dot(cv[rows], wc_ref[...], preferred_element_type=_F32)
        xn = x_ref[rows, :] + gate_ref[0] * y
        o_ref[rows, :] = xn * lax.rsqrt(jnp.mean(xn * xn, axis=-1, keepdims=True) + EPS) * gf_ref[...]


def _out_proj(attn, u, e, conv_w, w_out_bf, x2d, gate, gf, seq, tm):
    m, d = x2d.shape
    tiles_per_seq = seq // tm
    halo_per_tile = tm // HALO_ROWS
    n_halo = m // HALO_ROWS
    mix_spec = pl.BlockSpec((tm, ATTN_WIDTH), lambda i: (i, 0))
    return pl.pallas_call(
        functools.partial(_out_kernel, tiles_per_seq),
        grid=(m // tm,),
        in_specs=[mix_spec, mix_spec, mix_spec,
                  pl.BlockSpec((HALO_ROWS, CONV_WIDTH),
                               lambda i: (jnp.maximum(i * halo_per_tile - 1, 0), 0)),
                  pl.BlockSpec((HALO_ROWS, CONV_WIDTH),
                               lambda i: (jnp.minimum((i + 1) * halo_per_tile, n_halo - 1), 0)),
                  pl.BlockSpec((3, CONV_WIDTH), lambda i: (0, 0)),
                  pl.BlockSpec((ATTN_WIDTH, d), lambda i: (0, 0)),
                  pl.BlockSpec((CONV_WIDTH, d), lambda i: (1, 0)),
                  pl.BlockSpec((tm, d), lambda i: (i, 0)),
                  pl.BlockSpec((1, 1, d), lambda i: (i // tiles_per_seq, 0, 0)),
                  pl.BlockSpec((1, d), lambda i: (0, 0))],
        out_specs=pl.BlockSpec((tm, d), lambda i: (i, 0)),
        out_shape=jax.ShapeDtypeStruct((m, d), _F32),
        compiler_params=_params(1),
        name="out_proj",
    )(attn, u, e, u, u, conv_w, w_out_bf, w_out_bf, x2d, gate, gf)


def _rope_tables(seq):
    quarter = HEAD_DIM // 4
    t = np.arange(seq)
    row = (t // GRID_W).astype(np.float64)
    col = (t % GRID_W).astype(np.float64)
    inv = ROPE_THETA ** (-np.arange(quarter, dtype=np.float64) / quarter)
    ang = np.concatenate([row[:, None] * inv, row[:, None] * inv,
                          col[:, None] * inv, col[:, None] * inv], axis=1)
    lane = np.arange(HEAD_DIM) % (2 * quarter)
    sin = np.sin(ang)
    tables = (np.cos(ang), np.where(lane < quarter, -sin, 0.0), np.where(lane >= quarter, sin, 0.0))
    return tuple(jnp.asarray(tab.astype(np.float32)) for tab in tables)


def kernel(x, c, ctx, c_ctx, w_mod, b_mod, norm_g, w_in, q_norm_g, k_norm_g, conv_w, w_out,
           final_norm_g):
    batch, seq, d = x.shape
    ctx_len = ctx.shape[1]
    assert w_mod.shape[0] == 1, "single-layer kernel"
    assert seq % ROW_TILE == 0 and seq % KEY_CHUNK == 0 and seq % Q_TILE == 0

    cond = jnp.concatenate([c, c_ctx[None, :]], axis=0)
    cond_b = jnp.broadcast_to(cond[:, :, None], (batch + 1, d, LANES))
    mod = _adaln(cond_b, w_mod[0], b_mod[0][None, :])
    shift = mod[:batch + 1, None, 0 * d:1 * d]
    scale = mod[:batch + 1, None, 1 * d:2 * d]
    gate = mod[:batch, None, 2 * d:3 * d]

    g_in = norm_g[0][None, :]
    q_gain = (q_norm_g[0] * (math.log2(math.e) / math.sqrt(HEAD_DIM)))[None, :]
    k_gain = k_norm_g[0][None, :]

    ctx2d = ctx.reshape(batch * ctx_len, d)
    k_ctx, vt_ctx = _ctx_kv(ctx2d, g_in, scale[batch:], shift[batch:], w_in[0], k_gain, batch, ctx_len)

    x2d = x.reshape(batch * seq, d)
    tables = _rope_tables(seq)
    h, q, k, vt, ga, w_conv_bf = _proj_attn(x2d, g_in, scale[:batch], shift[:batch], w_in[0],
                                            q_gain, k_gain, tables, batch, seq, ROW_TILE)
    u, e, w_out_bf = _proj_conv(h, w_conv_bf, w_out[0], CONV_ROW_TILE)

    attn = _attention(q, k, vt, k_ctx, vt_ctx, ga, batch, seq)
    out = _out_proj(attn, u, e, conv_w[0], w_out_bf, x2d, gate, final_norm_g[None, :], seq, ROW_TILE)
    return out.reshape(batch, seq, d)
```

```python
import functools
import math

import jax
import jax.numpy as jnp
import numpy as np
from jax import lax
from jax.experimental import pallas as pl
from jax.experimental.pallas import tpu as pltpu

D_MODEL = 2048
CTX_LEN = 256
GRID_W = 64
ATTN_WIDTH = 1024
CONV_WIDTH = 1024
HEAD_DIM = 128
N_HEADS = 8
N_KV_HEADS = 2
GROUP = N_HEADS // N_KV_HEADS
KV_WIDTH = N_KV_HEADS * HEAD_DIM
ROPE_THETA = 10000.0
EPS = 1e-6

LANES = 128
ROW_TILE = 512
CONV_ROW_TILE = 1024
SUB_ROWS = 512
COL_TILE = 512
KEY_CHUNK = 512
Q_TILE = 256
V_ROWS = HEAD_DIM + 16
HALO_ROWS = 16
MOD_ROWS = 8
MOD_COL_TILE = 768
VMEM_LIMIT = 56 * 1024 * 1024
assert 2 * KV_WIDTH == COL_TILE and ATTN_WIDTH % COL_TILE == 0 and CONV_WIDTH % COL_TILE == 0

_F32 = jnp.float32
_BF16 = jnp.bfloat16


def _silu(v):
    return v * (1.0 / (1.0 + jnp.exp(-v)))


def _params(n_axes, flags=None):
    return pltpu.CompilerParams(dimension_semantics=("arbitrary",) * n_axes,
                                vmem_limit_bytes=VMEM_LIMIT, flags=flags)


def _adaln_kernel(cb_ref, w_ref, b_ref, o_ref, s_scr):
    @pl.when(pl.program_id(0) == 0)
    def _():
        s_scr[...] = _silu(cb_ref[...])

    n_cond, n_chunks = s_scr.shape[0], o_ref.shape[1] // LANES

    def slab(kb, accs):
        rows = pl.ds(pl.multiple_of(kb * 8, 8), 8)
        w8 = w_ref[rows, :]
        out = []
        for r in range(n_cond):
            s8 = s_scr[r, rows, :]
            for c in range(n_chunks):
                out.append(accs[r * n_chunks + c] + w8[:, c * LANES:(c + 1) * LANES] * s8)
        return tuple(out)

    zeros = tuple(jnp.zeros((8, LANES), _F32) for _ in range(n_cond * n_chunks))
    accs = lax.fori_loop(0, w_ref.shape[0] // 8, slab, zeros, unroll=8)
    o_ref[...] = jnp.zeros(o_ref.shape, _F32)
    for r in range(n_cond):
        for c in range(n_chunks):
            cols = slice(c * LANES, (c + 1) * LANES)
            o_ref[r:r + 1, cols] = accs[r * n_chunks + c].sum(axis=0, keepdims=True) + b_ref[:, cols]


def _adaln(cond_b, w_mod, b_mod):
    d, n = w_mod.shape
    return pl.pallas_call(
        _adaln_kernel,
        grid=(n // MOD_COL_TILE,),
        in_specs=[pl.BlockSpec((3, d, LANES), lambda j: (0, 0, 0)),
                  pl.BlockSpec((d, MOD_COL_TILE), lambda j: (0, j)),
                  pl.BlockSpec((1, MOD_COL_TILE), lambda j: (0, j))],
        out_specs=pl.BlockSpec((MOD_ROWS, MOD_COL_TILE), lambda j: (0, j)),
        out_shape=jax.ShapeDtypeStruct((MOD_ROWS, n), _F32),
        scratch_shapes=[pltpu.VMEM((3, d, LANES), _F32)],
        compiler_params=_params(1),
        name="adaln",
    )(cond_b, w_mod, b_mod)


def _modulated_norm(x_ref, gin_ref, scale_ref, shift_ref):
    xf = x_ref[...]
    y = xf * lax.rsqrt(jnp.mean(xf * xf, axis=-1, keepdims=True) + EPS) * gin_ref[...]
    return (y * (1.0 + scale_ref[0]) + shift_ref[0]).astype(_BF16)


def _norm_rope(xh, g, rope):
    y = xh * lax.rsqrt(jnp.mean(xh * xh, axis=-1, keepdims=True) + EPS) * g
    if rope is None:
        return y
    cos, sin_lo, sin_hi = rope
    return (y * cos + pltpu.roll(y, 3 * HEAD_DIM // 4, axis=1) * sin_lo
            + pltpu.roll(y, HEAD_DIM // 4, axis=1) * sin_hi)


def _store_kv(acc, g, rope, k_ref, vt_ref):
    for hh in range(N_KV_HEADS):
        xh = acc[:, hh * HEAD_DIM:(hh + 1) * HEAD_DIM]
        k_ref[0, hh] = _norm_rope(xh, g, rope).astype(_BF16)
        vh = acc[:, KV_WIDTH + hh * HEAD_DIM:KV_WIDTH + (hh + 1) * HEAD_DIM]
        vt_ref[0, hh, 0:HEAD_DIM] = vh.T.astype(_BF16)
        vt_ref[0, hh, HEAD_DIM:V_ROWS] = jnp.ones((V_ROWS - HEAD_DIM, vh.shape[0]), _BF16)


def _ctx_kv_kernel(x_ref, gin_ref, scale_ref, shift_ref, w_ref, g_ref, k_ref, vt_ref):
    h = _modulated_norm(x_ref, gin_ref, scale_ref, shift_ref)
    acc = jnp.dot(h, w_ref[...].astype(_BF16), preferred_element_type=_F32)
    _store_kv(acc, g_ref[...], None, k_ref, vt_ref)


def _proj_attn_kernel(x_ref, gin_ref, scale_ref, shift_ref, w_ref, qg_ref, kg_ref,
                      cos_ref, slo_ref, shi_ref, wside_ref, h_ref, q_ref, k_ref, vt_ref, ga_ref,
                      wside_bf_ref):
    wside_bf_ref[...] = wside_ref[...].astype(_BF16)
    h = _modulated_norm(x_ref, gin_ref, scale_ref, shift_ref)
    h_ref[...] = h
    rope = (cos_ref[...], slo_ref[...], shi_ref[...])
    heads_per_dot = COL_TILE // HEAD_DIM

    def project(col0):
        w = w_ref[:, col0:col0 + COL_TILE].astype(_BF16)
        return jnp.dot(h, w, preferred_element_type=_F32)

    for blk in range(ATTN_WIDTH // COL_TILE):
        acc = project(blk * COL_TILE)
        for hh in range(heads_per_dot):
            xh = acc[:, hh * HEAD_DIM:(hh + 1) * HEAD_DIM]
            q_ref[0, blk * heads_per_dot + hh] = _norm_rope(xh, qg_ref[...], rope).astype(_BF16)
    _store_kv(project(ATTN_WIDTH), kg_ref[...], rope, k_ref, vt_ref)
    gate_col = ATTN_WIDTH + 2 * KV_WIDTH
    for blk in range(ATTN_WIDTH // COL_TILE):
        acc = project(gate_col + blk * COL_TILE)
        ga_ref[:, blk * COL_TILE:(blk + 1) * COL_TILE] = _silu(acc).astype(_BF16)


def _proj_conv_kernel(tiles_per_seq, h_ref, wb_ref, wcg_ref, wh_ref, wgc_ref, cw_ref, wside_ref,
                      cv_ref, fix_ref, wside_bf_ref, u_last, e_last):
    i = pl.program_id(1)
    tm = h_ref.shape[0]
    wside_bf_ref[...] = wside_ref[...].astype(_BF16)

    @pl.when(i == 0)
    def _():
        u_last[...] = jnp.zeros(u_last.shape, _F32)
        e_last[...] = jnp.zeros(e_last.shape, _F32)

    def project(w_ref):
        return jnp.concatenate(
            [jnp.dot(h_ref[r0:r0 + SUB_ROWS, :], w_ref[...], preferred_element_type=_F32)
             for r0 in range(0, tm, SUB_ROWS)], axis=0)

    u = project(wcg_ref) * project(wh_ref)
    e = project(wb_ref) * _silu(project(wgc_ref))
    w0, w1, w2 = cw_ref[0:1, :], cw_ref[1:2, :], cw_ref[2:3, :]
    seq_start = (i % tiles_per_seq) == 0
    prev_row = jnp.where(seq_start, 0.0, u_last[...])
    slab_row = lax.broadcasted_iota(jnp.int32, (8, 1), 0)
    down, up = pltpu.roll(u, 1, axis=0), pltpu.roll(u, tm - 1, axis=0)
    u_before = jnp.concatenate([jnp.where(slab_row == 0, prev_row, down[0:8]), down[8:]], axis=0)
    u_after = jnp.concatenate([up[:tm - 8], jnp.where(slab_row == 7, 0.0, up[tm - 8:])], axis=0)
    cv_ref[...] = (e * (u_before * w0 + u * w1 + u_after * w2)).astype(_BF16)
    fix = jnp.where(seq_start, 0.0, e_last[...] * w2 * u[0:1])
    fix_ref[0] = jnp.where(slab_row == 0, fix, 0.0)
    u_last[...] = u[tm - 1:tm]
    e_last[...] = e[tm - 1:tm]


def _rope_specs(tm, seq):
    tiles = seq // tm
    return [pl.BlockSpec((tm, HEAD_DIM), lambda i, *_: (i % tiles, 0))] * 3


def _proj_attn(x2d, gin, scale, shift, w_in, q_gain, k_gain, tables, batch, seq, tm):
    m, d = x2d.shape
    tiles = seq // tm
    steps = m // tm
    n_cols = 2 * ATTN_WIDTH + 2 * KV_WIDTH
    side_cols = (w_in.shape[1] - n_cols) // steps
    assert side_cols % LANES == 0 and n_cols % side_cols == 0
    side_first = n_cols // side_cols
    mod_spec = pl.BlockSpec((1, 1, d), lambda i: (i // tiles, 0, 0))
    gain_spec = pl.BlockSpec((1, HEAD_DIM), lambda i: (0, 0))
    return pl.pallas_call(
        _proj_attn_kernel,
        grid=(steps,),
        in_specs=[pl.BlockSpec((tm, d), lambda i: (i, 0)),
                  pl.BlockSpec((1, d), lambda i: (0, 0)),
                  mod_spec, mod_spec,
                  pl.BlockSpec((d, n_cols), lambda i: (0, 0)),
                  gain_spec, gain_spec] + _rope_specs(tm, seq)
                 + [pl.BlockSpec((d, side_cols), lambda i: (0, side_first + i))],
        out_specs=[pl.BlockSpec((tm, d), lambda i: (i, 0)),
                   pl.BlockSpec((1, N_HEADS, tm, HEAD_DIM), lambda i: (i // tiles, 0, i % tiles, 0)),
                   pl.BlockSpec((1, N_KV_HEADS, tm, HEAD_DIM), lambda i: (i // tiles, 0, i % tiles, 0)),
                   pl.BlockSpec((1, N_KV_HEADS, V_ROWS, tm), lambda i: (i // tiles, 0, 0, i % tiles)),
                   pl.BlockSpec((tm, ATTN_WIDTH), lambda i: (i, 0)),
                   pl.BlockSpec((d, side_cols), lambda i: (0, i))],
        out_shape=[jax.ShapeDtypeStruct((m, d), _BF16),
                   jax.ShapeDtypeStruct((batch, N_HEADS, seq, HEAD_DIM), _BF16),
                   jax.ShapeDtypeStruct((batch, N_KV_HEADS, seq, HEAD_DIM), _BF16),
                   jax.ShapeDtypeStruct((batch, N_KV_HEADS, V_ROWS, seq), _BF16),
                   jax.ShapeDtypeStruct((m, ATTN_WIDTH), _BF16),
                   jax.ShapeDtypeStruct((d, side_cols * steps), _BF16)],
        compiler_params=_params(1),
        name="proj_attn",
    )(x2d, gin, scale, shift, w_in, q_gain, k_gain, *tables, w_in)


def _ctx_kv(ctx2d, gin, scale, shift, w_in, k_gain, batch, ctx_len):
    d = ctx2d.shape[1]
    kv_col_block = ATTN_WIDTH // COL_TILE
    mod_spec = pl.BlockSpec((1, 1, d), lambda i: (0, 0, 0))
    return pl.pallas_call(
        _ctx_kv_kernel,
        grid=(batch,),
        in_specs=[pl.BlockSpec((ctx_len, d), lambda i: (i, 0)),
                  pl.BlockSpec((1, d), lambda i: (0, 0)),
                  mod_spec, mod_spec,
                  pl.BlockSpec((d, 2 * KV_WIDTH), lambda i: (0, kv_col_block)),
                  pl.BlockSpec((1, HEAD_DIM), lambda i: (0, 0))],
        out_specs=[pl.BlockSpec((1, N_KV_HEADS, ctx_len, HEAD_DIM), lambda i: (i, 0, 0, 0)),
                   pl.BlockSpec((1, N_KV_HEADS, V_ROWS, ctx_len), lambda i: (i, 0, 0, 0))],
        out_shape=[jax.ShapeDtypeStruct((batch, N_KV_HEADS, ctx_len, HEAD_DIM), _BF16),
                   jax.ShapeDtypeStruct((batch, N_KV_HEADS, V_ROWS, ctx_len), _BF16)],
        compiler_params=_params(1),
        name="ctx_kv",
    )(ctx2d, gin, scale, shift, w_in, k_gain)


def _proj_conv(h, w_conv_bf, conv_w, w_out, seq, tm):
    m, d = h.shape
    step = CONV_WIDTH // COL_TILE
    row_tiles = m // tm
    side_rows = w_out.shape[0] // (step * row_tiles)
    assert side_rows % HALO_ROWS == 0 and seq % tm == 0

    def w_spec(group):
        return pl.BlockSpec((d, COL_TILE), lambda j, i: (0, group * step + j))

    side_spec = pl.BlockSpec((side_rows, w_out.shape[1]), lambda j, i: (j * row_tiles + i, 0))
    edge_row = pltpu.VMEM((1, COL_TILE), _F32)
    return pl.pallas_call(
        functools.partial(_proj_conv_kernel, seq // tm),
        grid=(step, row_tiles),
        in_specs=[pl.BlockSpec((tm, d), lambda j, i: (i, 0)),
                  w_spec(0), w_spec(1), w_spec(2), w_spec(3),
                  pl.BlockSpec((3, COL_TILE), lambda j, i: (0, j)), side_spec],
        out_specs=[pl.BlockSpec((tm, COL_TILE), lambda j, i: (i, j)),
                   pl.BlockSpec((1, 8, COL_TILE), lambda j, i: (i, 0, j)),
                   side_spec],
        out_shape=[jax.ShapeDtypeStruct((m, CONV_WIDTH), _BF16),
                   jax.ShapeDtypeStruct((row_tiles, 8, CONV_WIDTH), _F32),
                   jax.ShapeDtypeStruct(w_out.shape, _BF16)],
        scratch_shapes=[edge_row, edge_row],
        compiler_params=_params(2),
        name="proj_conv",
    )(h, w_conv_bf, w_conv_bf, w_conv_bf, w_conv_bf, conv_w, w_out)


def _attn_kernel(q_ref, k_ref, vt_ref, kc_ref, vtc_ref, ga_ref, o_ref, s_ring, m_ring, p_ring):
    n_chunks, kc = k_ref.shape[2], k_ref.shape[3]
    n_lat, ctx = n_chunks * kc, kc_ref.shape[2]
    tq = s_ring[0].shape[1]
    n_tiles = q_ref.shape[2] // tq
    nt = (((1,), (1,)), ((), ()))

    def fold(v):
        return v.reshape(v.shape[0] // 8, 8, v.shape[1])

    def rows(t):
        return pl.ds(pl.multiple_of(t * tq, tq), tq)

    def key_block(c):
        return (k_ref[0, 0, c], pl.ds(c * kc, kc)) if c < n_chunks else (kc_ref[0, 0], pl.ds(n_lat, ctx))

    def scores(t, g, c):
        keys, span = key_block(c)
        s = lax.dot_general(keys, q_ref[0, g, rows(t), :], nt, preferred_element_type=_F32)
        s_ring[g % 2][span, :] = s
        chunk_max = fold(s).max(axis=0)
        m_buf = m_ring[g % 2]
        m_buf[...] = chunk_max if c == 0 else jnp.maximum(m_buf[...], chunk_max)

    def probs(g, c, m):
        _, span = key_block(c)
        p_ring[g % 2][span, :] = jnp.exp2(s_ring[g % 2][span, :] - m).astype(_BF16)

    def values(t, g):
        p_buf = p_ring[g % 2]
        acc = jnp.dot(vt_ref[0, 0], p_buf[0:n_lat, :], preferred_element_type=_F32)
        acc += jnp.dot(vtc_ref[0, 0], p_buf[n_lat:n_lat + ctx, :], preferred_element_type=_F32)
        o = (acc[0:HEAD_DIM] * (1.0 / acc[HEAD_DIM:HEAD_DIM + 1])).T
        cols = slice(g * HEAD_DIM, (g + 1) * HEAD_DIM)
        o_ref[rows(t), cols] = (o * ga_ref[rows(t), cols].astype(_F32)).astype(_BF16)

    def phase(score_item, prob_head, value_item):
        if prob_head is not None:
            m = m_ring[prob_head % 2][...].max(axis=0, keepdims=True)
        for c in range(n_chunks + 1):
            if score_item is not None:
                scores(*score_item, c)
            if prob_head is not None:
                probs(prob_head, c, m)
        if value_item is not None:
            values(*value_item)

    assert GROUP == 4
    phase((0, 0), None, None)
    phase((0, 1), 0, None)

    def tile(t, carry):
        nxt = jnp.minimum(t + 1, n_tiles - 1)
        phase((t, 2), 1, (t, 0))
        phase((t, 3), 2, (t, 1))
        phase((nxt, 0), 3, (t, 2))
        phase((nxt, 1), 0, (t, 3))
        return carry

    lax.fori_loop(0, n_tiles, tile, 0, unroll=4)


def _attention(q, k, vt, k_ctx, vt_ctx, ga, batch, seq):
    n_chunks = seq // KEY_CHUNK
    ctx = k_ctx.shape[2]
    k5 = k.reshape(batch, N_KV_HEADS, n_chunks, KEY_CHUNK, HEAD_DIM)
    group_w = GROUP * HEAD_DIM
    n_keys = seq + ctx
    return pl.pallas_call(
        _attn_kernel,
        grid=(batch, N_KV_HEADS),
        in_specs=[pl.BlockSpec((1, GROUP, seq, HEAD_DIM), lambda b, kv: (b, kv, 0, 0)),
                  pl.BlockSpec((1, 1, n_chunks, KEY_CHUNK, HEAD_DIM), lambda b, kv: (b, kv, 0, 0, 0)),
                  pl.BlockSpec((1, 1, V_ROWS, seq), lambda b, kv: (b, kv, 0, 0)),
                  pl.BlockSpec((1, 1, ctx, HEAD_DIM), lambda b, kv: (b, kv, 0, 0)),
                  pl.BlockSpec((1, 1, V_ROWS, ctx), lambda b, kv: (b, kv, 0, 0)),
                  pl.BlockSpec((seq, group_w), lambda b, kv: (b, kv))],
        out_specs=pl.BlockSpec((seq, group_w), lambda b, kv: (b, kv)),
        out_shape=jax.ShapeDtypeStruct((batch * seq, ATTN_WIDTH), _BF16),
        scratch_shapes=[[pltpu.VMEM((n_keys, Q_TILE), _F32)] * 2,
                        [pltpu.VMEM((8, Q_TILE), _F32)] * 2,
                        [pltpu.VMEM((n_keys, Q_TILE), _BF16)] * 2],
        compiler_params=_params(2),
        name="attention",
    )(q, k5, vt, k_ctx, vt_ctx, ga)


def _out_kernel(fix_every, n_tiles, attn_ref, cv_ref, fix_ref, wa_ref, wc_ref, x_ref, gate_ref, gf_ref,
                o_ref):
    i = pl.program_id(0)
    tm = cv_ref.shape[0]
    slab_row = lax.broadcasted_iota(jnp.int32, (HALO_ROWS, 1), 0)
    ends_conv_tile = jnp.logical_and((i % fix_every) == fix_every - 1, i < n_tiles - 1)
    needs_fix = jnp.logical_and(slab_row == HALO_ROWS - 1, ends_conv_tile)
    tail = cv_ref[tm - HALO_ROWS:tm, :].astype(_F32) + jnp.where(needs_fix, fix_ref[0, 0:1, :], 0.0)
    cv = jnp.concatenate([cv_ref[0:tm - HALO_ROWS, :], tail.astype(_BF16)], axis=0)
    y = jnp.dot(attn_ref[...], wa_ref[...], preferred_element_type=_F32)
    y += jnp.dot(cv, wc_ref[...], preferred_element_type=_F32)
    xn = x_ref[...] + gate_ref[0] * y
    o_ref[...] = xn * lax.rsqrt(jnp.mean(xn * xn, axis=-1, keepdims=True) + EPS) * gf_ref[...]


def _out_proj(attn, cv, fix, w_out_bf, x2d, gate, gf, seq, tm):
    m, d = x2d.shape
    tiles_per_seq = seq // tm
    fix_every = (m // tm) // fix.shape[0]
    last_fix = fix.shape[0] - 1
    mix_spec = pl.BlockSpec((tm, ATTN_WIDTH), lambda i: (i, 0))
    return pl.pallas_call(
        functools.partial(_out_kernel, fix_every, m // tm),
        grid=(m // tm,),
        in_specs=[mix_spec, mix_spec,
                  pl.BlockSpec((1, 8, CONV_WIDTH),
                               lambda i: (jnp.minimum(i // fix_every + 1, last_fix), 0, 0)),
                  pl.BlockSpec((ATTN_WIDTH, d), lambda i: (0, 0)),
                  pl.BlockSpec((CONV_WIDTH, d), lambda i: (1, 0)),
                  pl.BlockSpec((tm, d), lambda i: (i, 0)),
                  pl.BlockSpec((1, 1, d), lambda i: (i // tiles_per_seq, 0, 0)),
                  pl.BlockSpec((1, d), lambda i: (0, 0))],
        out_specs=pl.BlockSpec((tm, d), lambda i: (i, 0)),
        out_shape=jax.ShapeDtypeStruct((m, d), _F32),
        compiler_params=_params(1),
        name="out_proj",
    )(attn, cv, fix, w_out_bf, w_out_bf, x2d, gate, gf)


def _rope_tables(seq):
    quarter = HEAD_DIM // 4
    t = np.arange(seq)
    row = (t // GRID_W).astype(np.float64)
    col = (t % GRID_W).astype(np.float64)
    inv = ROPE_THETA ** (-np.arange(quarter, dtype=np.float64) / quarter)
    ang = np.concatenate([row[:, None] * inv, row[:, None] * inv,
                          col[:, None] * inv, col[:, None] * inv], axis=1)
    lane = np.arange(HEAD_DIM) % (2 * quarter)
    sin = np.sin(ang)
    tables = (np.cos(ang), np.where(lane < quarter, -sin, 0.0), np.where(lane >= quarter, sin, 0.0))
    return tuple(jnp.asarray(tab.astype(np.float32)) for tab in tables)


def kernel(x, c, ctx, c_ctx, w_mod, b_mod, norm_g, w_in, q_norm_g, k_norm_g, conv_w, w_out,
           final_norm_g):
    batch, seq, d = x.shape
    ctx_len = ctx.shape[1]
    assert w_mod.shape[0] == 1, "single-layer kernel"
    assert seq % ROW_TILE == 0 and seq % KEY_CHUNK == 0 and seq % Q_TILE == 0

    cond = jnp.concatenate([c, c_ctx[None, :]], axis=0)
    cond_b = jnp.broadcast_to(cond[:, :, None], (batch + 1, d, LANES))
    mod = _adaln(cond_b, w_mod[0], b_mod[0][None, :])
    shift = mod[:batch + 1, None, 0 * d:1 * d]
    scale = mod[:batch + 1, None, 1 * d:2 * d]
    gate = mod[:batch, None, 2 * d:3 * d]

    g_in = norm_g[0][None, :]
    q_gain = (q_norm_g[0] * (math.log2(math.e) / math.sqrt(HEAD_DIM)))[None, :]
    k_gain = k_norm_g[0][None, :]

    ctx2d = ctx.reshape(batch * ctx_len, d)
    k_ctx, vt_ctx = _ctx_kv(ctx2d, g_in, scale[batch:], shift[batch:], w_in[0], k_gain, batch, ctx_len)

    x2d = x.reshape(batch * seq, d)
    tables = _rope_tables(seq)
    h, q, k, vt, ga, w_conv_bf = _proj_attn(x2d, g_in, scale[:batch], shift[:batch], w_in[0],
                                            q_gain, k_gain, tables, batch, seq, ROW_TILE)
    cv, fix, w_out_bf = _proj_conv(h, w_conv_bf, conv_w[0], w_out[0], seq, CONV_ROW_TILE)

    attn = _attention(q, k, vt, k_ctx, vt_ctx, ga, batch, seq)
    out = _out_proj(attn, cv, fix, w_out_bf, x2d, gate, final_norm_g[None, :], seq, ROW_TILE)
    return out.reshape(batch, seq, d)
```

```python
import functools
import math

import jax
import jax.numpy as jnp
import numpy as np
from jax import lax
from jax.experimental import pallas as pl
from jax.experimental.pallas import tpu as pltpu

D_MODEL = 2048
CTX_LEN = 256
GRID_W = 64
ATTN_WIDTH = 1024
CONV_WIDTH = 1024
HEAD_DIM = 128
N_HEADS = 8
N_KV_HEADS = 2
GROUP = N_HEADS // N_KV_HEADS
KV_WIDTH = N_KV_HEADS * HEAD_DIM
ROPE_THETA = 10000.0
EPS = 1e-6

LANES = 128
ROW_TILE = 512
CONV_ROW_TILE = 1024
SUB_ROWS = 512
COL_TILE = 512
KEY_CHUNK = 2048
Q_TILE = 256
V_ROWS = HEAD_DIM + 16
HALO_ROWS = 16
MOD_ROWS = 8
MOD_COL_TILE = 768
VMEM_LIMIT = 56 * 1024 * 1024
assert 2 * KV_WIDTH == COL_TILE and ATTN_WIDTH % COL_TILE == 0 and CONV_WIDTH % COL_TILE == 0

_F32 = jnp.float32
_BF16 = jnp.bfloat16


def _silu(v):
    return v * (1.0 / (1.0 + jnp.exp(-v)))


def _params(n_axes, flags=None):
    return pltpu.CompilerParams(dimension_semantics=("arbitrary",) * n_axes,
                                vmem_limit_bytes=VMEM_LIMIT, flags=flags)


def _adaln_kernel(cb_ref, w_ref, b_ref, o_ref, s_scr):
    @pl.when(pl.program_id(0) == 0)
    def _():
        s_scr[...] = _silu(cb_ref[...])

    n_cond, n_chunks = s_scr.shape[0], o_ref.shape[1] // LANES

    def slab(kb, accs):
        rows = pl.ds(pl.multiple_of(kb * 8, 8), 8)
        w8 = w_ref[rows, :]
        out = []
        for r in range(n_cond):
            s8 = s_scr[r, rows, :]
            for c in range(n_chunks):
                out.append(accs[r * n_chunks + c] + w8[:, c * LANES:(c + 1) * LANES] * s8)
        return tuple(out)

    zeros = tuple(jnp.zeros((8, LANES), _F32) for _ in range(n_cond * n_chunks))
    accs = lax.fori_loop(0, w_ref.shape[0] // 8, slab, zeros, unroll=8)
    o_ref[...] = jnp.zeros(o_ref.shape, _F32)
    for r in range(n_cond):
        for c in range(n_chunks):
            cols = slice(c * LANES, (c + 1) * LANES)
            o_ref[r:r + 1, cols] = accs[r * n_chunks + c].sum(axis=0, keepdims=True) + b_ref[:, cols]


def _adaln(cond_b, w_mod, b_mod):
    d, n = w_mod.shape
    return pl.pallas_call(
        _adaln_kernel,
        grid=(n // MOD_COL_TILE,),
        in_specs=[pl.BlockSpec((3, d, LANES), lambda j: (0, 0, 0)),
                  pl.BlockSpec((d, MOD_COL_TILE), lambda j: (0, j)),
                  pl.BlockSpec((1, MOD_COL_TILE), lambda j: (0, j))],
        out_specs=pl.BlockSpec((MOD_ROWS, MOD_COL_TILE), lambda j: (0, j)),
        out_shape=jax.ShapeDtypeStruct((MOD_ROWS, n), _F32),
        scratch_shapes=[pltpu.VMEM((3, d, LANES), _F32)],
        compiler_params=_params(1),
        name="adaln",
    )(cond_b, w_mod, b_mod)


def _modulated_norm(x_ref, gin_ref, scale_ref, shift_ref, rows=slice(None)):
    xf = x_ref[rows, :]
    y = xf * lax.rsqrt(jnp.mean(xf * xf, axis=-1, keepdims=True) + EPS) * gin_ref[...]
    return (y * (1.0 + scale_ref[0]) + shift_ref[0]).astype(_BF16)


def _norm_rope(xh, g, rope):
    y = xh * lax.rsqrt(jnp.mean(xh * xh, axis=-1, keepdims=True) + EPS) * g
    if rope is None:
        return y
    cos, sin_lo, sin_hi = rope
    return (y * cos + pltpu.roll(y, 3 * HEAD_DIM // 4, axis=1) * sin_lo
            + pltpu.roll(y, HEAD_DIM // 4, axis=1) * sin_hi)


def _store_kv(acc, g, rope, k_ref, vt_ref, rows=slice(None)):
    for hh in range(N_KV_HEADS):
        xh = acc[:, hh * HEAD_DIM:(hh + 1) * HEAD_DIM]
        k_ref[0, hh, rows, :] = _norm_rope(xh, g, rope).astype(_BF16)
        vh = acc[:, KV_WIDTH + hh * HEAD_DIM:KV_WIDTH + (hh + 1) * HEAD_DIM]
        vt_ref[0, hh, 0:HEAD_DIM, rows] = vh.T.astype(_BF16)
        vt_ref[0, hh, HEAD_DIM:V_ROWS, rows] = jnp.ones((V_ROWS - HEAD_DIM, vh.shape[0]), _BF16)


def _ctx_kv_kernel(x_ref, gin_ref, scale_ref, shift_ref, w_ref, g_ref, k_ref, vt_ref):
    h = _modulated_norm(x_ref, gin_ref, scale_ref, shift_ref)
    acc = jnp.dot(h, w_ref[...].astype(_BF16), preferred_element_type=_F32)
    _store_kv(acc, g_ref[...], None, k_ref, vt_ref)


def _proj_attn_kernel(x_ref, gin_ref, scale_ref, shift_ref, w_ref, qg_ref, kg_ref,
                      cos_ref, slo_ref, shi_ref, wside_ref, h_ref, q_ref, k_ref, vt_ref, ga_ref,
                      wside_bf_ref):
    wside_bf_ref[...] = wside_ref[...].astype(_BF16)
    half = x_ref.shape[0] // 2
    halves = [slice(r0, r0 + half) for r0 in range(0, x_ref.shape[0], half)]
    hs = []
    for rows in halves:
        h = _modulated_norm(x_ref, gin_ref, scale_ref, shift_ref, rows)
        h_ref[rows, :] = h
        hs.append(h)
    ropes = [(cos_ref[rows, :], slo_ref[rows, :], shi_ref[rows, :]) for rows in halves]
    heads_per_dot = COL_TILE // HEAD_DIM

    def project(col0):
        w = w_ref[:, col0:col0 + COL_TILE].astype(_BF16)
        return [jnp.dot(h, w, preferred_element_type=_F32) for h in hs]

    for blk in range(ATTN_WIDTH // COL_TILE):
        for rows, rope, acc in zip(halves, ropes, project(blk * COL_TILE)):
            for hh in range(heads_per_dot):
                xh = acc[:, hh * HEAD_DIM:(hh + 1) * HEAD_DIM]
                q_ref[0, blk * heads_per_dot + hh, rows, :] = _norm_rope(xh, qg_ref[...], rope).astype(_BF16)
    for rows, rope, acc in zip(halves, ropes, project(ATTN_WIDTH)):
        _store_kv(acc, kg_ref[...], rope, k_ref, vt_ref, rows)
    gate_col = ATTN_WIDTH + 2 * KV_WIDTH
    for blk in range(ATTN_WIDTH // COL_TILE):
        for rows, acc in zip(halves, project(gate_col + blk * COL_TILE)):
            ga_ref[rows, blk * COL_TILE:(blk + 1) * COL_TILE] = _silu(acc).astype(_BF16)


def _proj_conv_kernel(tiles_per_seq, h_ref, wb_ref, wcg_ref, wh_ref, wgc_ref, cw_ref, wside_ref,
                      cv_ref, fix_ref, wside_bf_ref, u_last, e_last):
    i = pl.program_id(1)
    tm = h_ref.shape[0]
    wside_bf_ref[...] = wside_ref[...].astype(_BF16)

    @pl.when(i == 0)
    def _():
        u_last[...] = jnp.zeros(u_last.shape, _F32)
        e_last[...] = jnp.zeros(e_last.shape, _F32)

    def project(w_ref):
        return jnp.concatenate(
            [jnp.dot(h_ref[r0:r0 + SUB_ROWS, :], w_ref[...], preferred_element_type=_F32)
             for r0 in range(0, tm, SUB_ROWS)], axis=0)

    u = project(wcg_ref) * project(wh_ref)
    e = project(wb_ref) * _silu(project(wgc_ref))
    w0, w1, w2 = cw_ref[0:1, :], cw_ref[1:2, :], cw_ref[2:3, :]
    seq_start = (i % tiles_per_seq) == 0
    prev_row = jnp.where(seq_start, 0.0, u_last[...])
    slab_row = lax.broadcasted_iota(jnp.int32, (8, 1), 0)
    down, up = pltpu.roll(u, 1, axis=0), pltpu.roll(u, tm - 1, axis=0)
    u_before = jnp.concatenate([jnp.where(slab_row == 0, prev_row, down[0:8]), down[8:]], axis=0)
    u_after = jnp.concatenate([up[:tm - 8], jnp.where(slab_row == 7, 0.0, up[tm - 8:])], axis=0)
    cv_ref[...] = (e * (u_before * w0 + u * w1 + u_after * w2)).astype(_BF16)
    fix = jnp.where(seq_start, 0.0, e_last[...] * w2 * u[0:1])
    fix_ref[0] = jnp.where(slab_row == 0, fix, 0.0)
    u_last[...] = u[tm - 1:tm]
    e_last[...] = e[tm - 1:tm]


def _rope_specs(tm, seq):
    tiles = seq // tm
    return [pl.BlockSpec((tm, HEAD_DIM), lambda i, *_: (i % tiles, 0))] * 3


def _proj_attn(x2d, gin, scale, shift, w_in, q_gain, k_gain, tables, batch, seq, tm):
    m, d = x2d.shape
    tiles = seq // tm
    steps = m // tm
    n_cols = 2 * ATTN_WIDTH + 2 * KV_WIDTH
    side_cols = (w_in.shape[1] - n_cols) // steps
    assert side_cols % LANES == 0 and n_cols % side_cols == 0
    side_first = n_cols // side_cols
    mod_spec = pl.BlockSpec((1, 1, d), lambda i: (i // tiles, 0, 0))
    gain_spec = pl.BlockSpec((1, HEAD_DIM), lambda i: (0, 0))
    return pl.pallas_call(
        _proj_attn_kernel,
        grid=(steps,),
        in_specs=[pl.BlockSpec((tm, d), lambda i: (i, 0)),
                  pl.BlockSpec((1, d), lambda i: (0, 0)),
                  mod_spec, mod_spec,
                  pl.BlockSpec((d, n_cols), lambda i: (0, 0)),
                  gain_spec, gain_spec] + _rope_specs(tm, seq)
                 + [pl.BlockSpec((d, side_cols), lambda i: (0, side_first + i))],
        out_specs=[pl.BlockSpec((tm, d), lambda i: (i, 0)),
                   pl.BlockSpec((1, N_HEADS, tm, HEAD_DIM), lambda i: (i // tiles, 0, i % tiles, 0)),
                   pl.BlockSpec((1, N_KV_HEADS, tm, HEAD_DIM), lambda i: (i // tiles, 0, i % tiles, 0)),
                   pl.BlockSpec((1, N_KV_HEADS, V_ROWS, tm), lambda i: (i // tiles, 0, 0, i % tiles)),
                   pl.BlockSpec((tm, ATTN_WIDTH), lambda i: (i, 0)),
                   pl.BlockSpec((d, side_cols), lambda i: (0, i))],
        out_shape=[jax.ShapeDtypeStruct((m, d), _BF16),
                   jax.ShapeDtypeStruct((batch, N_HEADS, seq, HEAD_DIM), _BF16),
                   jax.ShapeDtypeStruct((batch, N_KV_HEADS, seq, HEAD_DIM), _BF16),
                   jax.ShapeDtypeStruct((batch, N_KV_HEADS, V_ROWS, seq), _BF16),
                   jax.ShapeDtypeStruct((m, ATTN_WIDTH), _BF16),
                   jax.ShapeDtypeStruct((d, side_cols * steps), _BF16)],
        compiler_params=_params(1),
        name="proj_attn",
    )(x2d, gin, scale, shift, w_in, q_gain, k_gain, *tables, w_in)


def _ctx_kv(ctx2d, gin, scale, shift, w_in, k_gain, batch, ctx_len):
    d = ctx2d.shape[1]
    kv_col_block = ATTN_WIDTH // COL_TILE
    mod_spec = pl.BlockSpec((1, 1, d), lambda i: (0, 0, 0))
    return pl.pallas_call(
        _ctx_kv_kernel,
        grid=(batch,),
        in_specs=[pl.BlockSpec((ctx_len, d), lambda i: (i, 0)),
                  pl.BlockSpec((1, d), lambda i: (0, 0)),
                  mod_spec, mod_spec,
                  pl.BlockSpec((d, 2 * KV_WIDTH), lambda i: (0, kv_col_block)),
                  pl.BlockSpec((1, HEAD_DIM), lambda i: (0, 0))],
        out_specs=[pl.BlockSpec((1, N_KV_HEADS, ctx_len, HEAD_DIM), lambda i: (i, 0, 0, 0)),
                   pl.BlockSpec((1, N_KV_HEADS, V_ROWS, ctx_len), lambda i: (i, 0, 0, 0))],
        out_shape=[jax.ShapeDtypeStruct((batch, N_KV_HEADS, ctx_len, HEAD_DIM), _BF16),
                   jax.ShapeDtypeStruct((batch, N_KV_HEADS, V_ROWS, ctx_len), _BF16)],
        compiler_params=_params(1),
        name="ctx_kv",
    )(ctx2d, gin, scale, shift, w_in, k_gain)


def _proj_conv(h, w_conv_bf, conv_w, w_out, seq, tm):
    m, d = h.shape
    step = CONV_WIDTH // COL_TILE
    row_tiles = m // tm
    side_rows = w_out.shape[0] // (step * row_tiles)
    assert side_rows % HALO_ROWS == 0 and seq % tm == 0

    def w_spec(group):
        return pl.BlockSpec((d, COL_TILE), lambda j, i: (0, group * step + j))

    side_spec = pl.BlockSpec((side_rows, w_out.shape[1]), lambda j, i: (j * row_tiles + i, 0))
    edge_row = pltpu.VMEM((1, COL_TILE), _F32)
    return pl.pallas_call(
        functools.partial(_proj_conv_kernel, seq // tm),
        grid=(step, row_tiles),
        in_specs=[pl.BlockSpec((tm, d), lambda j, i: (i, 0)),
                  w_spec(0), w_spec(1), w_spec(2), w_spec(3),
                  pl.BlockSpec((3, COL_TILE), lambda j, i: (0, j)), side_spec],
        out_specs=[pl.BlockSpec((tm, COL_TILE), lambda j, i: (i, j)),
                   pl.BlockSpec((1, 8, COL_TILE), lambda j, i: (i, 0, j)),
                   side_spec],
        out_shape=[jax.ShapeDtypeStruct((m, CONV_WIDTH), _BF16),
                   jax.ShapeDtypeStruct((row_tiles, 8, CONV_WIDTH), _F32),
                   jax.ShapeDtypeStruct(w_out.shape, _BF16)],
        scratch_shapes=[edge_row, edge_row],
        compiler_params=_params(2),
        name="proj_conv",
    )(h, w_conv_bf, w_conv_bf, w_conv_bf, w_conv_bf, conv_w, w_out)


def _attn_kernel(q_ref, k_ref, vt_ref, kc_ref, vtc_ref, ga_ref, o_ref, s_ring, m_ring, p_ring):
    n_chunks, kc = k_ref.shape[2], k_ref.shape[3]
    n_lat, ctx = n_chunks * kc, kc_ref.shape[2]
    tq = s_ring[0].shape[1]
    n_tiles = q_ref.shape[2] // tq
    nt = (((1,), (1,)), ((), ()))

    def fold(v):
        return v.reshape(v.shape[0] // 8, 8, v.shape[1])

    def rows(t):
        return pl.ds(pl.multiple_of(t * tq, tq), tq)

    def key_block(c):
        return (k_ref[0, 0, c], pl.ds(c * kc, kc)) if c < n_chunks else (kc_ref[0, 0], pl.ds(n_lat, ctx))

    def scores(t, g, c):
        keys, span = key_block(c)
        s = lax.dot_general(keys, q_ref[0, g, rows(t), :], nt, preferred_element_type=_F32)
        s_ring[g % 2][span, :] = s
        chunk_max = fold(s).max(axis=0)
        m_buf = m_ring[g % 2]
        m_buf[...] = chunk_max if c == 0 else jnp.maximum(m_buf[...], chunk_max)

    def probs(g, c, m):
        _, span = key_block(c)
        p_ring[g % 2][span, :] = jnp.exp2(s_ring[g % 2][span, :] - m).astype(_BF16)

    def values(t, g):
        p_buf = p_ring[g % 2]
        acc = jnp.dot(vt_ref[0, 0], p_buf[0:n_lat, :], preferred_element_type=_F32)
        acc += jnp.dot(vtc_ref[0, 0], p_buf[n_lat:n_lat + ctx, :], preferred_element_type=_F32)
        o = (acc[0:HEAD_DIM] * (1.0 / acc[HEAD_DIM:HEAD_DIM + 1])).T
        cols = slice(g * HEAD_DIM, (g + 1) * HEAD_DIM)
        o_ref[rows(t), cols] = (o * ga_ref[rows(t), cols].astype(_F32)).astype(_BF16)

    def phase(score_item, prob_head, value_item):
        if prob_head is not None:
            m = m_ring[prob_head % 2][...].max(axis=0, keepdims=True)
        for c in range(n_chunks + 1):
            if score_item is not None:
                scores(*score_item, c)
            if prob_head is not None:
                probs(prob_head, c, m)
        if value_item is not None:
            values(*value_item)

    assert GROUP == 4
    phase((0, 0), None, None)
    phase((0, 1), 0, None)

    def tile(t, carry):
        nxt = jnp.minimum(t + 1, n_tiles - 1)
        phase((t, 2), 1, (t, 0))
        phase((t, 3), 2, (t, 1))
        phase((nxt, 0), 3, (t, 2))
        phase((nxt, 1), 0, (t, 3))
        return carry

    lax.fori_loop(0, n_tiles, tile, 0, unroll=4)


def _attention(q, k, vt, k_ctx, vt_ctx, ga, batch, seq):
    n_chunks = seq // KEY_CHUNK
    ctx = k_ctx.shape[2]
    k5 = k.reshape(batch, N_KV_HEADS, n_chunks, KEY_CHUNK, HEAD_DIM)
    group_w = GROUP * HEAD_DIM
    n_keys = seq + ctx
    return pl.pallas_call(
        _attn_kernel,
        grid=(batch, N_KV_HEADS),
        in_specs=[pl.BlockSpec((1, GROUP, seq, HEAD_DIM), lambda b, kv: (b, kv, 0, 0)),
                  pl.BlockSpec((1, 1, n_chunks, KEY_CHUNK, HEAD_DIM), lambda b, kv: (b, kv, 0, 0, 0)),
                  pl.BlockSpec((1, 1, V_ROWS, seq), lambda b, kv: (b, kv, 0, 0)),
                  pl.BlockSpec((1, 1, ctx, HEAD_DIM), lambda b, kv: (b, kv, 0, 0)),
                  pl.BlockSpec((1, 1, V_ROWS, ctx), lambda b, kv: (b, kv, 0, 0)),
                  pl.BlockSpec((seq, group_w), lambda b, kv: (b, kv))],
        out_specs=pl.BlockSpec((seq, group_w), lambda b, kv: (b, kv)),
        out_shape=jax.ShapeDtypeStruct((batch * seq, ATTN_WIDTH), _BF16),
        scratch_shapes=[[pltpu.VMEM((n_keys, Q_TILE), _F32)] * 2,
                        [pltpu.VMEM((8, Q_TILE), _F32)] * 2,
                        [pltpu.VMEM((n_keys, Q_TILE), _BF16)] * 2],
        compiler_params=_params(2),
        name="attention",
    )(q, k5, vt, k_ctx, vt_ctx, ga)


def _out_kernel(fix_every, n_tiles, attn_ref, cv_ref, fix_ref, wa_ref, wc_ref, x_ref, gate_ref, gf_ref,
                o_ref):
    i = pl.program_id(0)
    tm = cv_ref.shape[0]
    slab_row = lax.broadcasted_iota(jnp.int32, (HALO_ROWS, 1), 0)
    ends_conv_tile = jnp.logical_and((i % fix_every) == fix_every - 1, i < n_tiles - 1)
    needs_fix = jnp.logical_and(slab_row == HALO_ROWS - 1, ends_conv_tile)
    tail = cv_ref[tm - HALO_ROWS:tm, :].astype(_F32) + jnp.where(needs_fix, fix_ref[0, 0:1, :], 0.0)
    cv = jnp.concatenate([cv_ref[0:tm - HALO_ROWS, :], tail.astype(_BF16)], axis=0)
    y = jnp.dot(attn_ref[...], wa_ref[...], preferred_element_type=_F32)
    y += jnp.dot(cv, wc_ref[...], preferred_element_type=_F32)
    xn = x_ref[...] + gate_ref[0] * y
    o_ref[...] = xn * lax.rsqrt(jnp.mean(xn * xn, axis=-1, keepdims=True) + EPS) * gf_ref[...]


def _out_proj(attn, cv, fix, w_out_bf, x2d, gate, gf, seq, tm):
    m, d = x2d.shape
    tiles_per_seq = seq // tm
    fix_every = (m // tm) // fix.shape[0]
    last_fix = fix.shape[0] - 1
    mix_spec = pl.BlockSpec((tm, ATTN_WIDTH), lambda i: (i, 0))
    return pl.pallas_call(
        functools.partial(_out_kernel, fix_every, m // tm),
        grid=(m // tm,),
        in_specs=[mix_spec, mix_spec,
                  pl.BlockSpec((1, 8, CONV_WIDTH),
                               lambda i: (jnp.minimum(i // fix_every + 1, last_fix), 0, 0)),
                  pl.BlockSpec((ATTN_WIDTH, d), lambda i: (0, 0)),
                  pl.BlockSpec((CONV_WIDTH, d), lambda i: (1, 0)),
                  pl.BlockSpec((tm, d), lambda i: (i, 0)),
                  pl.BlockSpec((1, 1, d), lambda i: (i // tiles_per_seq, 0, 0)),
                  pl.BlockSpec((1, d), lambda i: (0, 0))],
        out_specs=pl.BlockSpec((tm, d), lambda i: (i, 0)),
        out_shape=jax.ShapeDtypeStruct((m, d), _F32),
        compiler_params=_params(1),
        name="out_proj",
    )(attn, cv, fix, w_out_bf, w_out_bf, x2d, gate, gf)


def _rope_tables(seq):
    quarter = HEAD_DIM // 4
    t = np.arange(seq)
    row = (t // GRID_W).astype(np.float64)
    col = (t % GRID_W).astype(np.float64)
    inv = ROPE_THETA ** (-np.arange(quarter, dtype=np.float64) / quarter)
    ang = np.concatenate([row[:, None] * inv, row[:, None] * inv,
                          col[:, None] * inv, col[:, None] * inv], axis=1)
    lane = np.arange(HEAD_DIM) % (2 * quarter)
    sin = np.sin(ang)
    tables = (np.cos(ang), np.where(lane < quarter, -sin, 0.0), np.where(lane >= quarter, sin, 0.0))
    return tuple(jnp.asarray(tab.astype(np.float32)) for tab in tables)


def kernel(x, c, ctx, c_ctx, w_mod, b_mod, norm_g, w_in, q_norm_g, k_norm_g, conv_w, w_out,
           final_norm_g):
    batch, seq, d = x.shape
    ctx_len = ctx.shape[1]
    assert w_mod.shape[0] == 1, "single-layer kernel"
    assert seq % ROW_TILE == 0 and seq % KEY_CHUNK == 0 and seq % Q_TILE == 0

    cond = jnp.concatenate([c, c_ctx[None, :]], axis=0)
    cond_b = jnp.broadcast_to(cond[:, :, None], (batch + 1, d, LANES))
    mod = _adaln(cond_b, w_mod[0], b_mod[0][None, :])
    shift = mod[:batch + 1, None, 0 * d:1 * d]
    scale = mod[:batch + 1, None, 1 * d:2 * d]
    gate = mod[:batch, None, 2 * d:3 * d]

    g_in = norm_g[0][None, :]
    q_gain = (q_norm_g[0] * (math.log2(math.e) / math.sqrt(HEAD_DIM)))[None, :]
    k_gain = k_norm_g[0][None, :]

    ctx2d = ctx.reshape(batch * ctx_len, d)
    k_ctx, vt_ctx = _ctx_kv(ctx2d, g_in, scale[batch:], shift[batch:], w_in[0], k_gain, batch, ctx_len)

    x2d = x.reshape(batch * seq, d)
    tables = _rope_tables(seq)
    h, q, k, vt, ga, w_conv_bf = _proj_attn(x2d, g_in, scale[:batch], shift[:batch], w_in[0],
                                            q_gain, k_gain, tables, batch, seq, ROW_TILE)
    cv, fix, w_out_bf = _proj_conv(h, w_conv_bf, conv_w[0], w_out[0], seq, CONV_ROW_TILE)

    attn = _attention(q, k, vt, k_ctx, vt_ctx, ga, batch, seq)
    out = _out_proj(attn, cv, fix, w_out_bf, x2d, gate, final_norm_g[None, :], seq, ROW_TILE)
    return out.reshape(batch, seq, d)
```

```python
import functools
import math

import jax
import jax.numpy as jnp
import numpy as np
from jax import lax
from jax.experimental import pallas as pl
from jax.experimental.pallas import tpu as pltpu

D_MODEL = 2048
CTX_LEN = 256
GRID_W = 64
ATTN_WIDTH = 1024
CONV_WIDTH = 1024
HEAD_DIM = 128
N_HEADS = 8
N_KV_HEADS = 2
GROUP = N_HEADS // N_KV_HEADS
KV_WIDTH = N_KV_HEADS * HEAD_DIM
ROPE_THETA = 10000.0
EPS = 1e-6

LANES = 128
ROW_TILE = 512
CONV_ROW_TILE = 1024
SUB_ROWS = 512
COL_TILE = 512
KEY_CHUNK = 2048
Q_TILE = 256
V_ROWS = HEAD_DIM + 16
HALO_ROWS = 16
MOD_ROWS = 8
MOD_COL_TILE = 768
VMEM_LIMIT = 56 * 1024 * 1024
assert 2 * KV_WIDTH == COL_TILE and ATTN_WIDTH % COL_TILE == 0 and CONV_WIDTH % COL_TILE == 0

_F32 = jnp.float32
_BF16 = jnp.bfloat16


def _silu(v):
    return v * (1.0 / (1.0 + jnp.exp(-v)))


def _params(n_axes, flags=None):
    return pltpu.CompilerParams(dimension_semantics=("arbitrary",) * n_axes,
                                vmem_limit_bytes=VMEM_LIMIT, flags=flags)


def _adaln_kernel(cond_ref, w_ref, b_ref, o_ref):
    s = _silu(cond_ref[...]).astype(_BF16)
    o_ref[...] = jnp.dot(s, w_ref[...].astype(_BF16), preferred_element_type=_F32) + b_ref[...]


def _adaln(cond_rows, w_mod, b_mod):
    d, n = w_mod.shape
    return pl.pallas_call(
        _adaln_kernel,
        grid=(n // MOD_COL_TILE,),
        in_specs=[pl.BlockSpec((MOD_ROWS, d), lambda j: (0, 0)),
                  pl.BlockSpec((d, MOD_COL_TILE), lambda j: (0, j)),
                  pl.BlockSpec((1, MOD_COL_TILE), lambda j: (0, j))],
        out_specs=pl.BlockSpec((MOD_ROWS, MOD_COL_TILE), lambda j: (0, j)),
        out_shape=jax.ShapeDtypeStruct((MOD_ROWS, n), _F32),
        compiler_params=_params(1),
        name="adaln",
    )(cond_rows, w_mod, b_mod)


def _modulated_norm(x_ref, gin_ref, scale_ref, shift_ref, rows=slice(None)):
    xf = x_ref[rows, :]
    y = xf * lax.rsqrt(jnp.mean(xf * xf, axis=-1, keepdims=True) + EPS) * gin_ref[...]
    return (y * (1.0 + scale_ref[0]) + shift_ref[0]).astype(_BF16)


def _norm_rope(xh, g, rope):
    y = xh * lax.rsqrt(jnp.mean(xh * xh, axis=-1, keepdims=True) + EPS) * g
    if rope is None:
        return y
    cos, sin_lo, sin_hi = rope
    return (y * cos + pltpu.roll(y, 3 * HEAD_DIM // 4, axis=1) * sin_lo
            + pltpu.roll(y, HEAD_DIM // 4, axis=1) * sin_hi)


def _store_kv(acc, g, rope, k_ref, vt_ref, rows=slice(None)):
    for hh in range(N_KV_HEADS):
        xh = acc[:, hh * HEAD_DIM:(hh + 1) * HEAD_DIM]
        k_ref[0, hh, rows, :] = _norm_rope(xh, g, rope).astype(_BF16)
        vh = acc[:, KV_WIDTH + hh * HEAD_DIM:KV_WIDTH + (hh + 1) * HEAD_DIM]
        vt_ref[0, hh, 0:HEAD_DIM, rows] = vh.T.astype(_BF16)
        vt_ref[0, hh, HEAD_DIM:V_ROWS, rows] = jnp.ones((V_ROWS - HEAD_DIM, vh.shape[0]), _BF16)


def _ctx_kv_kernel(x_ref, gin_ref, scale_ref, shift_ref, w_ref, g_ref, k_ref, vt_ref):
    h = _modulated_norm(x_ref, gin_ref, scale_ref, shift_ref)
    acc = jnp.dot(h, w_ref[...].astype(_BF16), preferred_element_type=_F32)
    _store_kv(acc, g_ref[...], None, k_ref, vt_ref)


def _proj_attn_kernel(x_ref, gin_ref, scale_ref, shift_ref, w_ref, qg_ref, kg_ref,
                      cos_ref, slo_ref, shi_ref, wside_ref, h_ref, q_ref, k_ref, vt_ref, ga_ref,
                      wside_bf_ref):
    wside_bf_ref[...] = wside_ref[...].astype(_BF16)
    half = x_ref.shape[0] // 2
    halves = [slice(r0, r0 + half) for r0 in range(0, x_ref.shape[0], half)]
    hs = []
    for rows in halves:
        h = _modulated_norm(x_ref, gin_ref, scale_ref, shift_ref, rows)
        h_ref[rows, :] = h
        hs.append(h)
    ropes = [(cos_ref[rows, :], slo_ref[rows, :], shi_ref[rows, :]) for rows in halves]
    heads_per_dot = COL_TILE // HEAD_DIM

    def project(col0):
        w = w_ref[:, col0:col0 + COL_TILE].astype(_BF16)
        return [jnp.dot(h, w, preferred_element_type=_F32) for h in hs]

    for blk in range(ATTN_WIDTH // COL_TILE):
        for rows, rope, acc in zip(halves, ropes, project(blk * COL_TILE)):
            for hh in range(heads_per_dot):
                xh = acc[:, hh * HEAD_DIM:(hh + 1) * HEAD_DIM]
                q_ref[0, blk * heads_per_dot + hh, rows, :] = _norm_rope(xh, qg_ref[...], rope).astype(_BF16)
    for rows, rope, acc in zip(halves, ropes, project(ATTN_WIDTH)):
        _store_kv(acc, kg_ref[...], rope, k_ref, vt_ref, rows)
    gate_col = ATTN_WIDTH + 2 * KV_WIDTH
    for blk in range(ATTN_WIDTH // COL_TILE):
        for rows, acc in zip(halves, project(gate_col + blk * COL_TILE)):
            ga_ref[rows, blk * COL_TILE:(blk + 1) * COL_TILE] = _silu(acc).astype(_BF16)


def _proj_conv_kernel(tiles_per_seq, h_ref, wb_ref, wcg_ref, wh_ref, wgc_ref, cw_ref, wside_ref,
                      cv_ref, fix_ref, wside_bf_ref, u_last, e_last):
    i = pl.program_id(1)
    tm = h_ref.shape[0]
    wside_bf_ref[...] = wside_ref[...].astype(_BF16)

    @pl.when(i == 0)
    def _():
        u_last[...] = jnp.zeros(u_last.shape, _F32)
        e_last[...] = jnp.zeros(e_last.shape, _F32)

    def project(w_ref):
        return jnp.concatenate(
            [jnp.dot(h_ref[r0:r0 + SUB_ROWS, :], w_ref[...], preferred_element_type=_F32)
             for r0 in range(0, tm, SUB_ROWS)], axis=0)

    u = project(wcg_ref) * project(wh_ref)
    e = project(wb_ref) * _silu(project(wgc_ref))
    w0, w1, w2 = cw_ref[0:1, :], cw_ref[1:2, :], cw_ref[2:3, :]
    seq_start = (i % tiles_per_seq) == 0
    prev_row = jnp.where(seq_start, 0.0, u_last[...])
    slab_row = lax.broadcasted_iota(jnp.int32, (8, 1), 0)
    down, up = pltpu.roll(u, 1, axis=0), pltpu.roll(u, tm - 1, axis=0)
    u_before = jnp.concatenate([jnp.where(slab_row == 0, prev_row, down[0:8]), down[8:]], axis=0)
    u_after = jnp.concatenate([up[:tm - 8], jnp.where(slab_row == 7, 0.0, up[tm - 8:])], axis=0)
    cv_ref[...] = (e * (u_before * w0 + u * w1 + u_after * w2)).astype(_BF16)
    fix = jnp.where(seq_start, 0.0, e_last[...] * w2 * u[0:1])
    fix_ref[0] = jnp.where(slab_row == 0, fix, 0.0)
    u_last[...] = u[tm - 1:tm]
    e_last[...] = e[tm - 1:tm]


def _rope_specs(tm, seq):
    tiles = seq // tm
    return [pl.BlockSpec((tm, HEAD_DIM), lambda i, *_: (i % tiles, 0))] * 3


def _proj_attn(x2d, gin, scale, shift, w_in, q_gain, k_gain, tables, batch, seq, tm):
    m, d = x2d.shape
    tiles = seq // tm
    steps = m // tm
    n_cols = 2 * ATTN_WIDTH + 2 * KV_WIDTH
    side_cols = (w_in.shape[1] - n_cols) // steps
    assert side_cols % LANES == 0 and n_cols % side_cols == 0
    side_first = n_cols // side_cols
    mod_spec = pl.BlockSpec((1, 1, d), lambda i: (i // tiles, 0, 0))
    gain_spec = pl.BlockSpec((1, HEAD_DIM), lambda i: (0, 0))
    return pl.pallas_call(
        _proj_attn_kernel,
        grid=(steps,),
        in_specs=[pl.BlockSpec((tm, d), lambda i: (i, 0)),
                  pl.BlockSpec((1, d), lambda i: (0, 0)),
                  mod_spec, mod_spec,
                  pl.BlockSpec((d, n_cols), lambda i: (0, 0)),
                  gain_spec, gain_spec] + _rope_specs(tm, seq)
                 + [pl.BlockSpec((d, side_cols), lambda i: (0, side_first + i))],
        out_specs=[pl.BlockSpec((tm, d), lambda i: (i, 0)),
                   pl.BlockSpec((1, N_HEADS, tm, HEAD_DIM), lambda i: (i // tiles, 0, i % tiles, 0)),
                   pl.BlockSpec((1, N_KV_HEADS, tm, HEAD_DIM), lambda i: (i // tiles, 0, i % tiles, 0)),
                   pl.BlockSpec((1, N_KV_HEADS, V_ROWS, tm), lambda i: (i // tiles, 0, 0, i % tiles)),
                   pl.BlockSpec((tm, ATTN_WIDTH), lambda i: (i, 0)),
                   pl.BlockSpec((d, side_cols), lambda i: (0, i))],
        out_shape=[jax.ShapeDtypeStruct((m, d), _BF16),
                   jax.ShapeDtypeStruct((batch, N_HEADS, seq, HEAD_DIM), _BF16),
                   jax.ShapeDtypeStruct((batch, N_KV_HEADS, seq, HEAD_DIM), _BF16),
                   jax.ShapeDtypeStruct((batch, N_KV_HEADS, V_ROWS, seq), _BF16),
                   jax.ShapeDtypeStruct((m, ATTN_WIDTH), _BF16),
                   jax.ShapeDtypeStruct((d, side_cols * steps), _BF16)],
        compiler_params=_params(1),
        name="proj_attn",
    )(x2d, gin, scale, shift, w_in, q_gain, k_gain, *tables, w_in)


def _ctx_kv(ctx2d, gin, scale, shift, w_in, k_gain, batch, ctx_len):
    d = ctx2d.shape[1]
    kv_col_block = ATTN_WIDTH // COL_TILE
    mod_spec = pl.BlockSpec((1, 1, d), lambda i: (0, 0, 0))
    return pl.pallas_call(
        _ctx_kv_kernel,
        grid=(batch,),
        in_specs=[pl.BlockSpec((ctx_len, d), lambda i: (i, 0)),
                  pl.BlockSpec((1, d), lambda i: (0, 0)),
                  mod_spec, mod_spec,
                  pl.BlockSpec((d, 2 * KV_WIDTH), lambda i: (0, kv_col_block)),
                  pl.BlockSpec((1, HEAD_DIM), lambda i: (0, 0))],
        out_specs=[pl.BlockSpec((1, N_KV_HEADS, ctx_len, HEAD_DIM), lambda i: (i, 0, 0, 0)),
                   pl.BlockSpec((1, N_KV_HEADS, V_ROWS, ctx_len), lambda i: (i, 0, 0, 0))],
        out_shape=[jax.ShapeDtypeStruct((batch, N_KV_HEADS, ctx_len, HEAD_DIM), _BF16),
                   jax.ShapeDtypeStruct((batch, N_KV_HEADS, V_ROWS, ctx_len), _BF16)],
        compiler_params=_params(1),
        name="ctx_kv",
    )(ctx2d, gin, scale, shift, w_in, k_gain)


def _proj_conv(h, w_conv_bf, conv_w, w_out, seq, tm):
    m, d = h.shape
    step = CONV_WIDTH // COL_TILE
    row_tiles = m // tm
    side_rows = w_out.shape[0] // (step * row_tiles)
    assert side_rows % HALO_ROWS == 0 and seq % tm == 0

    def w_spec(group):
        return pl.BlockSpec((d, COL_TILE), lambda j, i: (0, group * step + j))

    side_spec = pl.BlockSpec((side_rows, w_out.shape[1]), lambda j, i: (j * row_tiles + i, 0))
    edge_row = pltpu.VMEM((1, COL_TILE), _F32)
    return pl.pallas_call(
        functools.partial(_proj_conv_kernel, seq // tm),
        grid=(step, row_tiles),
        in_specs=[pl.BlockSpec((tm, d), lambda j, i: (i, 0)),
                  w_spec(0), w_spec(1), w_spec(2), w_spec(3),
                  pl.BlockSpec((3, COL_TILE), lambda j, i: (0, j)), side_spec],
        out_specs=[pl.BlockSpec((tm, COL_TILE), lambda j, i: (i, j)),
                   pl.BlockSpec((1, 8, COL_TILE), lambda j, i: (i, 0, j)),
                   side_spec],
        out_shape=[jax.ShapeDtypeStruct((m, CONV_WIDTH), _BF16),
                   jax.ShapeDtypeStruct((row_tiles, 8, CONV_WIDTH), _F32),
                   jax.ShapeDtypeStruct(w_out.shape, _BF16)],
        scratch_shapes=[edge_row, edge_row],
        compiler_params=_params(2),
        name="proj_conv",
    )(h, w_conv_bf, w_conv_bf, w_conv_bf, w_conv_bf, conv_w, w_out)


def _attn_kernel(q_ref, k_ref, vt_ref, kc_ref, vtc_ref, ga_ref, o_ref, s_ring, m_ring, p_ring):
    n_chunks, kc = k_ref.shape[2], k_ref.shape[3]
    n_lat, ctx = n_chunks * kc, kc_ref.shape[2]
    tq = s_ring[0].shape[1]
    n_tiles = q_ref.shape[2] // tq
    nt = (((1,), (1,)), ((), ()))

    def fold(v):
        return v.reshape(v.shape[0] // 8, 8, v.shape[1])

    def rows(t):
        return pl.ds(pl.multiple_of(t * tq, tq), tq)

    def key_block(c):
        return (k_ref[0, 0, c], pl.ds(c * kc, kc)) if c < n_chunks else (kc_ref[0, 0], pl.ds(n_lat, ctx))

    def scores(t, g, c):
        keys, span = key_block(c)
        s = lax.dot_general(keys, q_ref[0, g, rows(t), :], nt, preferred_element_type=_F32)
        s_ring[g % 2][span, :] = s
        chunk_max = fold(s).max(axis=0)
        m_buf = m_ring[g % 2]
        m_buf[...] = chunk_max if c == 0 else jnp.maximum(m_buf[...], chunk_max)

    def probs(g, c, m):
        _, span = key_block(c)
        p_ring[g % 2][span, :] = jnp.exp2(s_ring[g % 2][span, :] - m).astype(_BF16)

    def values(t, g):
        p_buf = p_ring[g % 2]
        acc = jnp.dot(vt_ref[0, 0], p_buf[0:n_lat, :], preferred_element_type=_F32)
        acc += jnp.dot(vtc_ref[0, 0], p_buf[n_lat:n_lat + ctx, :], preferred_element_type=_F32)
        o = (acc[0:HEAD_DIM] * (1.0 / acc[HEAD_DIM:HEAD_DIM + 1])).T
        cols = slice(g * HEAD_DIM, (g + 1) * HEAD_DIM)
        o_ref[rows(t), cols] = (o * ga_ref[rows(t), cols].astype(_F32)).astype(_BF16)

    def phase(score_item, prob_head, value_item):
        if prob_head is not None:
            m = m_ring[prob_head % 2][...].max(axis=0, keepdims=True)
        for c in range(n_chunks + 1):
            if score_item is not None:
                scores(*score_item, c)
            if prob_head is not None:
                probs(prob_head, c, m)
        if value_item is not None:
            values(*value_item)

    assert GROUP == 4
    phase((0, 0), None, None)
    phase((0, 1), 0, None)

    def tile(t, carry):
        nxt = jnp.minimum(t + 1, n_tiles - 1)
        phase((t, 2), 1, (t, 0))
        phase((t, 3), 2, (t, 1))
        phase((nxt, 0), 3, (t, 2))
        phase((nxt, 1), 0, (t, 3))
        return carry

    lax.fori_loop(0, n_tiles, tile, 0, unroll=4)


def _attention(q, k, vt, k_ctx, vt_ctx, ga, batch, seq):
    n_chunks = seq // KEY_CHUNK
    ctx = k_ctx.shape[2]
    k5 = k.reshape(batch, N_KV_HEADS, n_chunks, KEY_CHUNK, HEAD_DIM)
    group_w = GROUP * HEAD_DIM
    n_keys = seq + ctx
    return pl.pallas_call(
        _attn_kernel,
        grid=(batch, N_KV_HEADS),
        in_specs=[pl.BlockSpec((1, GROUP, seq, HEAD_DIM), lambda b, kv: (b, kv, 0, 0)),
                  pl.BlockSpec((1, 1, n_chunks, KEY_CHUNK, HEAD_DIM), lambda b, kv: (b, kv, 0, 0, 0)),
                  pl.BlockSpec((1, 1, V_ROWS, seq), lambda b, kv: (b, kv, 0, 0)),
                  pl.BlockSpec((1, 1, ctx, HEAD_DIM), lambda b, kv: (b, kv, 0, 0)),
                  pl.BlockSpec((1, 1, V_ROWS, ctx), lambda b, kv: (b, kv, 0, 0)),
                  pl.BlockSpec((seq, group_w), lambda b, kv: (b, kv))],
        out_specs=pl.BlockSpec((seq, group_w), lambda b, kv: (b, kv)),
        out_shape=jax.ShapeDtypeStruct((batch * seq, ATTN_WIDTH), _BF16),
        scratch_shapes=[[pltpu.VMEM((n_keys, Q_TILE), _F32)] * 2,
                        [pltpu.VMEM((8, Q_TILE), _F32)] * 2,
                        [pltpu.VMEM((n_keys, Q_TILE), _BF16)] * 2],
        compiler_params=_params(2),
        name="attention",
    )(q, k5, vt, k_ctx, vt_ctx, ga)


def _out_kernel(fix_every, n_tiles, attn_ref, cv_ref, fix_ref, wa_ref, wc_ref, x_ref, gate_ref, gf_ref,
                o_ref):
    i = pl.program_id(0)
    tm = cv_ref.shape[0]
    slab_row = lax.broadcasted_iota(jnp.int32, (HALO_ROWS, 1), 0)
    ends_conv_tile = jnp.logical_and((i % fix_every) == fix_every - 1, i < n_tiles - 1)
    needs_fix = jnp.logical_and(slab_row == HALO_ROWS - 1, ends_conv_tile)
    tail = cv_ref[tm - HALO_ROWS:tm, :].astype(_F32) + jnp.where(needs_fix, fix_ref[0, 0:1, :], 0.0)
    cv = jnp.concatenate([cv_ref[0:tm - HALO_ROWS, :], tail.astype(_BF16)], axis=0)
    y = jnp.dot(attn_ref[...], wa_ref[...], preferred_element_type=_F32)
    y += jnp.dot(cv, wc_ref[...], preferred_element_type=_F32)
    xn = x_ref[...] + gate_ref[0] * y
    o_ref[...] = xn * lax.rsqrt(jnp.mean(xn * xn, axis=-1, keepdims=True) + EPS) * gf_ref[...]


def _out_proj(attn, cv, fix, w_out_bf, x2d, gate, gf, seq, tm):
    m, d = x2d.shape
    tiles_per_seq = seq // tm
    fix_every = (m // tm) // fix.shape[0]
    last_fix = fix.shape[0] - 1
    mix_spec = pl.BlockSpec((tm, ATTN_WIDTH), lambda i: (i, 0))
    return pl.pallas_call(
        functools.partial(_out_kernel, fix_every, m // tm),
        grid=(m // tm,),
        in_specs=[mix_spec, mix_spec,
                  pl.BlockSpec((1, 8, CONV_WIDTH),
                               lambda i: (jnp.minimum(i // fix_every + 1, last_fix), 0, 0)),
                  pl.BlockSpec((ATTN_WIDTH, d), lambda i: (0, 0)),
                  pl.BlockSpec((CONV_WIDTH, d), lambda i: (1, 0)),
                  pl.BlockSpec((tm, d), lambda i: (i, 0)),
                  pl.BlockSpec((1, 1, d), lambda i: (i // tiles_per_seq, 0, 0)),
                  pl.BlockSpec((1, d), lambda i: (0, 0))],
        out_specs=pl.BlockSpec((tm, d), lambda i: (i, 0)),
        out_shape=jax.ShapeDtypeStruct((m, d), _F32),
        compiler_params=_params(1),
        name="out_proj",
    )(attn, cv, fix, w_out_bf, w_out_bf, x2d, gate, gf)


def _rope_tables(seq):
    quarter = HEAD_DIM // 4
    t = np.arange(seq)
    row = (t // GRID_W).astype(np.float64)
    col = (t % GRID_W).astype(np.float64)
    inv = ROPE_THETA ** (-np.arange(quarter, dtype=np.float64) / quarter)
    ang = np.concatenate([row[:, None] * inv, row[:, None] * inv,
                          col[:, None] * inv, col[:, None] * inv], axis=1)
    lane = np.arange(HEAD_DIM) % (2 * quarter)
    sin = np.sin(ang)
    tables = (np.cos(ang), np.where(lane < quarter, -sin, 0.0), np.where(lane >= quarter, sin, 0.0))
    return tuple(jnp.asarray(tab.astype(np.float32)) for tab in tables)


def kernel(x, c, ctx, c_ctx, w_mod, b_mod, norm_g, w_in, q_norm_g, k_norm_g, conv_w, w_out,
           final_norm_g):
    batch, seq, d = x.shape
    ctx_len = ctx.shape[1]
    assert w_mod.shape[0] == 1, "single-layer kernel"
    assert seq % ROW_TILE == 0 and seq % KEY_CHUNK == 0 and seq % Q_TILE == 0

    cond_rows = jnp.concatenate([c, c_ctx[None, :], jnp.zeros((MOD_ROWS - batch - 1, d), _F32)], axis=0)
    mod = _adaln(cond_rows, w_mod[0], b_mod[0][None, :])
    shift = mod[:batch + 1, None, 0 * d:1 * d]
    scale = mod[:batch + 1, None, 1 * d:2 * d]
    gate = mod[:batch, None, 2 * d:3 * d]

    g_in = norm_g[0][None, :]
    q_gain = (q_norm_g[0] * (math.log2(math.e) / math.sqrt(HEAD_DIM)))[None, :]
    k_gain = k_norm_g[0][None, :]

    ctx2d = ctx.reshape(batch * ctx_len, d)
    k_ctx, vt_ctx = _ctx_kv(ctx2d, g_in, scale[batch:], shift[batch:], w_in[0], k_gain, batch, ctx_len)

    x2d = x.reshape(batch * seq, d)
    tables = _rope_tables(seq)
    h, q, k, vt, ga, w_conv_bf = _proj_attn(x2d, g_in, scale[:batch], shift[:batch], w_in[0],
                                            q_gain, k_gain, tables, batch, seq, ROW_TILE)
    cv, fix, w_out_bf = _proj_conv(h, w_conv_bf, conv_w[0], w_out[0], seq, CONV_ROW_TILE)

    attn = _attention(q, k, vt, k_ctx, vt_ctx, ga, batch, seq)
    out = _out_proj(attn, cv, fix, w_out_bf, x2d, gate, final_norm_g[None, :], seq, ROW_TILE)
    return out.reshape(batch, seq, d)
```

```python
import functools
import math

import jax
import jax.numpy as jnp
import numpy as np
from jax import lax
from jax.experimental import pallas as pl
from jax.experimental.pallas import tpu as pltpu

D_MODEL = 2048
CTX_LEN = 256
GRID_W = 64
ATTN_WIDTH = 1024
CONV_WIDTH = 1024
HEAD_DIM = 128
N_HEADS = 8
N_KV_HEADS = 2
GROUP = N_HEADS // N_KV_HEADS
KV_WIDTH = N_KV_HEADS * HEAD_DIM
ROPE_THETA = 10000.0
EPS = 1e-6

LANES = 128
ROW_TILE = 512
CONV_ROW_TILE = 1024
SUB_ROWS = 512
COL_TILE = 512
KEY_CHUNK = 2048
Q_TILE = 256
V_ROWS = HEAD_DIM + 16
HALO_ROWS = 16
MOD_ROWS = 8
MOD_ROW_TILE = 256
VMEM_LIMIT = 56 * 1024 * 1024
assert 2 * KV_WIDTH == COL_TILE and ATTN_WIDTH % COL_TILE == 0 and CONV_WIDTH % COL_TILE == 0

_F32 = jnp.float32
_BF16 = jnp.bfloat16


def _silu(v):
    return v * (1.0 / (1.0 + jnp.exp(-v)))


def _params(n_axes, flags=None):
    return pltpu.CompilerParams(dimension_semantics=("arbitrary",) * n_axes,
                                vmem_limit_bytes=VMEM_LIMIT, flags=flags)


def _adaln_kernel(cond_ref, w_ref, b_ref, o_ref):
    @pl.when(pl.program_id(0) == 0)
    def _():
        o_ref[...] = jnp.broadcast_to(b_ref[...], o_ref.shape)

    s = _silu(cond_ref[0]).astype(_BF16)
    o_ref[...] += jnp.dot(s, w_ref[...].astype(_BF16), preferred_element_type=_F32)


def _adaln(cond_rows, w_mod, b_mod):
    d, n = w_mod.shape
    bands = d // MOD_ROW_TILE
    cond_bands = cond_rows.reshape(MOD_ROWS, bands, MOD_ROW_TILE).transpose(1, 0, 2)
    return pl.pallas_call(
        _adaln_kernel,
        grid=(bands,),
        in_specs=[pl.BlockSpec((1, MOD_ROWS, MOD_ROW_TILE), lambda k: (k, 0, 0)),
                  pl.BlockSpec((MOD_ROW_TILE, n), lambda k: (k, 0)),
                  pl.BlockSpec((1, n), lambda k: (0, 0))],
        out_specs=pl.BlockSpec((MOD_ROWS, n), lambda k: (0, 0)),
        out_shape=jax.ShapeDtypeStruct((MOD_ROWS, n), _F32),
        compiler_params=_params(1),
        name="adaln",
    )(cond_bands, w_mod, b_mod)


def _modulated_norm(x_ref, gin_ref, scale_ref, shift_ref, rows=slice(None)):
    xf = x_ref[rows, :]
    y = xf * lax.rsqrt(jnp.mean(xf * xf, axis=-1, keepdims=True) + EPS) * gin_ref[...]
    return (y * (1.0 + scale_ref[0]) + shift_ref[0]).astype(_BF16)


def _norm_rope(xh, g, rope):
    y = xh * lax.rsqrt(jnp.mean(xh * xh, axis=-1, keepdims=True) + EPS) * g
    if rope is None:
        return y
    cos, sin_lo, sin_hi = rope
    return (y * cos + pltpu.roll(y, 3 * HEAD_DIM // 4, axis=1) * sin_lo
            + pltpu.roll(y, HEAD_DIM // 4, axis=1) * sin_hi)


def _store_kv(acc, g, rope, k_ref, vt_ref, rows=slice(None)):
    for hh in range(N_KV_HEADS):
        xh = acc[:, hh * HEAD_DIM:(hh + 1) * HEAD_DIM]
        k_ref[0, hh, rows, :] = _norm_rope(xh, g, rope).astype(_BF16)
        vh = acc[:, KV_WIDTH + hh * HEAD_DIM:KV_WIDTH + (hh + 1) * HEAD_DIM]
        vt_ref[0, hh, 0:HEAD_DIM, rows] = vh.T.astype(_BF16)
        vt_ref[0, hh, HEAD_DIM:V_ROWS, rows] = jnp.ones((V_ROWS - HEAD_DIM, vh.shape[0]), _BF16)


def _ctx_kv_kernel(x_ref, gin_ref, scale_ref, shift_ref, w_ref, g_ref, k_ref, vt_ref):
    h = _modulated_norm(x_ref, gin_ref, scale_ref, shift_ref)
    acc = jnp.dot(h, w_ref[...].astype(_BF16), preferred_element_type=_F32)
    _store_kv(acc, g_ref[...], None, k_ref, vt_ref)


def _proj_attn_kernel(x_ref, gin_ref, scale_ref, shift_ref, w_ref, qg_ref, kg_ref,
                      cos_ref, slo_ref, shi_ref, wside_ref, h_ref, q_ref, k_ref, vt_ref, ga_ref,
                      wside_bf_ref):
    wside_bf_ref[...] = wside_ref[...].astype(_BF16)
    half = x_ref.shape[0] // 2
    halves = [slice(r0, r0 + half) for r0 in range(0, x_ref.shape[0], half)]
    hs = []
    for rows in halves:
        h = _modulated_norm(x_ref, gin_ref, scale_ref, shift_ref, rows)
        h_ref[rows, :] = h
        hs.append(h)
    ropes = [(cos_ref[rows, :], slo_ref[rows, :], shi_ref[rows, :]) for rows in halves]
    heads_per_dot = COL_TILE // HEAD_DIM

    def project(col0):
        w = w_ref[:, col0:col0 + COL_TILE].astype(_BF16)
        return [jnp.dot(h, w, preferred_element_type=_F32) for h in hs]

    for blk in range(ATTN_WIDTH // COL_TILE):
        for rows, rope, acc in zip(halves, ropes, project(blk * COL_TILE)):
            for hh in range(heads_per_dot):
                xh = acc[:, hh * HEAD_DIM:(hh + 1) * HEAD_DIM]
                q_ref[0, blk * heads_per_dot + hh, rows, :] = _norm_rope(xh, qg_ref[...], rope).astype(_BF16)
    for rows, rope, acc in zip(halves, ropes, project(ATTN_WIDTH)):
        _store_kv(acc, kg_ref[...], rope, k_ref, vt_ref, rows)
    gate_col = ATTN_WIDTH + 2 * KV_WIDTH
    for blk in range(ATTN_WIDTH // COL_TILE):
        for rows, acc in zip(halves, project(gate_col + blk * COL_TILE)):
            ga_ref[rows, blk * COL_TILE:(blk + 1) * COL_TILE] = _silu(acc).astype(_BF16)


def _proj_conv_kernel(tiles_per_seq, h_ref, wb_ref, wcg_ref, wh_ref, wgc_ref, cw_ref, wside_ref,
                      cv_ref, fix_ref, wside_bf_ref, u_last, e_last):
    i = pl.program_id(1)
    tm = h_ref.shape[0]
    wside_bf_ref[...] = wside_ref[...].astype(_BF16)

    @pl.when(i == 0)
    def _():
        u_last[...] = jnp.zeros(u_last.shape, _F32)
        e_last[...] = jnp.zeros(e_last.shape, _F32)

    def project(w_ref):
        return jnp.concatenate(
            [jnp.dot(h_ref[r0:r0 + SUB_ROWS, :], w_ref[...], preferred_element_type=_F32)
             for r0 in range(0, tm, SUB_ROWS)], axis=0)

    u = project(wcg_ref) * project(wh_ref)
    e = project(wb_ref) * _silu(project(wgc_ref))
    w0, w1, w2 = cw_ref[0:1, :], cw_ref[1:2, :], cw_ref[2:3, :]
    seq_start = (i % tiles_per_seq) == 0
    prev_row = jnp.where(seq_start, 0.0, u_last[...])
    slab_row = lax.broadcasted_iota(jnp.int32, (8, 1), 0)
    down, up = pltpu.roll(u, 1, axis=0), pltpu.roll(u, tm - 1, axis=0)
    u_before = jnp.concatenate([jnp.where(slab_row == 0, prev_row, down[0:8]), down[8:]], axis=0)
    u_after = jnp.concatenate([up[:tm - 8], jnp.where(slab_row == 7, 0.0, up[tm - 8:])], axis=0)
    cv_ref[...] = (e * (u_before * w0 + u * w1 + u_after * w2)).astype(_BF16)
    fix = jnp.where(seq_start, 0.0, e_last[...] * w2 * u[0:1])
    fix_ref[0] = jnp.where(slab_row == 0, fix, 0.0)
    u_last[...] = u[tm - 1:tm]
    e_last[...] = e[tm - 1:tm]


def _rope_specs(tm, seq):
    tiles = seq // tm
    return [pl.BlockSpec((tm, HEAD_DIM), lambda i, *_: (i % tiles, 0))] * 3


def _proj_attn(x2d, gin, scale, shift, w_in, q_gain, k_gain, tables, batch, seq, tm):
    m, d = x2d.shape
    tiles = seq // tm
    steps = m // tm
    n_cols = 2 * ATTN_WIDTH + 2 * KV_WIDTH
    side_cols = (w_in.shape[1] - n_cols) // steps
    assert side_cols % LANES == 0 and n_cols % side_cols == 0
    side_first = n_cols // side_cols
    mod_spec = pl.BlockSpec((1, 1, d), lambda i: (i // tiles, 0, 0))
    gain_spec = pl.BlockSpec((1, HEAD_DIM), lambda i: (0, 0))
    return pl.pallas_call(
        _proj_attn_kernel,
        grid=(steps,),
        in_specs=[pl.BlockSpec((tm, d), lambda i: (i, 0)),
                  pl.BlockSpec((1, d), lambda i: (0, 0)),
                  mod_spec, mod_spec,
                  pl.BlockSpec((d, n_cols), lambda i: (0, 0)),
                  gain_spec, gain_spec] + _rope_specs(tm, seq)
                 + [pl.BlockSpec((d, side_cols), lambda i: (0, side_first + i))],
        out_specs=[pl.BlockSpec((tm, d), lambda i: (i, 0)),
                   pl.BlockSpec((1, N_HEADS, tm, HEAD_DIM), lambda i: (i // tiles, 0, i % tiles, 0)),
                   pl.BlockSpec((1, N_KV_HEADS, tm, HEAD_DIM), lambda i: (i // tiles, 0, i % tiles, 0)),
                   pl.BlockSpec((1, N_KV_HEADS, V_ROWS, tm), lambda i: (i // tiles, 0, 0, i % tiles)),
                   pl.BlockSpec((tm, ATTN_WIDTH), lambda i: (i, 0)),
                   pl.BlockSpec((d, side_cols), lambda i: (0, i))],
        out_shape=[jax.ShapeDtypeStruct((m, d), _BF16),
                   jax.ShapeDtypeStruct((batch, N_HEADS, seq, HEAD_DIM), _BF16),
                   jax.ShapeDtypeStruct((batch, N_KV_HEADS, seq, HEAD_DIM), _BF16),
                   jax.ShapeDtypeStruct((batch, N_KV_HEADS, V_ROWS, seq), _BF16),
                   jax.ShapeDtypeStruct((m, ATTN_WIDTH), _BF16),
                   jax.ShapeDtypeStruct((d, side_cols * steps), _BF16)],
        compiler_params=_params(1),
        name="proj_attn",
    )(x2d, gin, scale, shift, w_in, q_gain, k_gain, *tables, w_in)


def _ctx_kv(ctx2d, gin, scale, shift, w_in, k_gain, batch, ctx_len):
    d = ctx2d.shape[1]
    kv_col_block = ATTN_WIDTH // COL_TILE
    mod_spec = pl.BlockSpec((1, 1, d), lambda i: (0, 0, 0))
    return pl.pallas_call(
        _ctx_kv_kernel,
        grid=(batch,),
        in_specs=[pl.BlockSpec((ctx_len, d), lambda i: (i, 0)),
                  pl.BlockSpec((1, d), lambda i: (0, 0)),
                  mod_spec, mod_spec,
                  pl.BlockSpec((d, 2 * KV_WIDTH), lambda i: (0, kv_col_block)),
                  pl.BlockSpec((1, HEAD_DIM), lambda i: (0, 0))],
        out_specs=[pl.BlockSpec((1, N_KV_HEADS, ctx_len, HEAD_DIM), lambda i: (i, 0, 0, 0)),
                   pl.BlockSpec((1, N_KV_HEADS, V_ROWS, ctx_len), lambda i: (i, 0, 0, 0))],
        out_shape=[jax.ShapeDtypeStruct((batch, N_KV_HEADS, ctx_len, HEAD_DIM), _BF16),
                   jax.ShapeDtypeStruct((batch, N_KV_HEADS, V_ROWS, ctx_len), _BF16)],
        compiler_params=_params(1),
        name="ctx_kv",
    )(ctx2d, gin, scale, shift, w_in, k_gain)


def _proj_conv(h, w_conv_bf, conv_w, w_out, seq, tm):
    m, d = h.shape
    step = CONV_WIDTH // COL_TILE
    row_tiles = m // tm
    side_rows = w_out.shape[0] // (step * row_tiles)
    assert side_rows % HALO_ROWS == 0 and seq % tm == 0

    def w_spec(group):
        return pl.BlockSpec((d, COL_TILE), lambda j, i: (0, group * step + j))

    side_spec = pl.BlockSpec((side_rows, w_out.shape[1]), lambda j, i: (j * row_tiles + i, 0))
    edge_row = pltpu.VMEM((1, COL_TILE), _F32)
    return pl.pallas_call(
        functools.partial(_proj_conv_kernel, seq // tm),
        grid=(step, row_tiles),
        in_specs=[pl.BlockSpec((tm, d), lambda j, i: (i, 0)),
                  w_spec(0), w_spec(1), w_spec(2), w_spec(3),
                  pl.BlockSpec((3, COL_TILE), lambda j, i: (0, j)), side_spec],
        out_specs=[pl.BlockSpec((tm, COL_TILE), lambda j, i: (i, j)),
                   pl.BlockSpec((1, 8, COL_TILE), lambda j, i: (i, 0, j)),
                   side_spec],
        out_shape=[jax.ShapeDtypeStruct((m, CONV_WIDTH), _BF16),
                   jax.ShapeDtypeStruct((row_tiles, 8, CONV_WIDTH), _F32),
                   jax.ShapeDtypeStruct(w_out.shape, _BF16)],
        scratch_shapes=[edge_row, edge_row],
        compiler_params=_params(2),
        name="proj_conv",
    )(h, w_conv_bf, w_conv_bf, w_conv_bf, w_conv_bf, conv_w, w_out)


def _attn_kernel(q_ref, k_ref, vt_ref, kc_ref, vtc_ref, ga_ref, o_ref, s_ring, m_ring, p_ring):
    n_chunks, kc = k_ref.shape[2], k_ref.shape[3]
    n_lat, ctx = n_chunks * kc, kc_ref.shape[2]
    tq = s_ring[0].shape[1]
    n_tiles = q_ref.shape[2] // tq
    nt = (((1,), (1,)), ((), ()))

    def fold(v):
        return v.reshape(v.shape[0] // 8, 8, v.shape[1])

    def rows(t):
        return pl.ds(pl.multiple_of(t * tq, tq), tq)

    def key_block(c):
        return (k_ref[0, 0, c], pl.ds(c * kc, kc)) if c < n_chunks else (kc_ref[0, 0], pl.ds(n_lat, ctx))

    def scores(t, g, c):
        keys, span = key_block(c)
        s = lax.dot_general(keys, q_ref[0, g, rows(t), :], nt, preferred_element_type=_F32)
        s_ring[g % 2][span, :] = s
        chunk_max = fold(s).max(axis=0)
        m_buf = m_ring[g % 2]
        m_buf[...] = chunk_max if c == 0 else jnp.maximum(m_buf[...], chunk_max)

    def probs(g, c, m):
        _, span = key_block(c)
        p_ring[g % 2][span, :] = jnp.exp2(s_ring[g % 2][span, :] - m).astype(_BF16)

    def values(t, g):
        p_buf = p_ring[g % 2]
        acc = jnp.dot(vt_ref[0, 0], p_buf[0:n_lat, :], preferred_element_type=_F32)
        acc += jnp.dot(vtc_ref[0, 0], p_buf[n_lat:n_lat + ctx, :], preferred_element_type=_F32)
        o = (acc[0:HEAD_DIM] * (1.0 / acc[HEAD_DIM:HEAD_DIM + 1])).T
        cols = slice(g * HEAD_DIM, (g + 1) * HEAD_DIM)
        o_ref[rows(t), cols] = (o * ga_ref[rows(t), cols].astype(_F32)).astype(_BF16)

    def phase(score_item, prob_head, value_item):
        if prob_head is not None:
            m = m_ring[prob_head % 2][...].max(axis=0, keepdims=True)
        for c in range(n_chunks + 1):
            if score_item is not None:
                scores(*score_item, c)
            if prob_head is not None:
                probs(prob_head, c, m)
        if value_item is not None:
            values(*value_item)

    assert GROUP == 4
    phase((0, 0), None, None)
    phase((0, 1), 0, None)

    def tile(t, carry):
        nxt = jnp.minimum(t + 1, n_tiles - 1)
        phase((t, 2), 1, (t, 0))
        phase((t, 3), 2, (t, 1))
        phase((nxt, 0), 3, (t, 2))
        phase((nxt, 1), 0, (t, 3))
        return carry

    lax.fori_loop(0, n_tiles, tile, 0, unroll=4)


def _attention(q, k, vt, k_ctx, vt_ctx, ga, batch, seq):
    n_chunks = seq // KEY_CHUNK
    ctx = k_ctx.shape[2]
    k5 = k.reshape(batch, N_KV_HEADS, n_chunks, KEY_CHUNK, HEAD_DIM)
    group_w = GROUP * HEAD_DIM
    n_keys = seq + ctx
    return pl.pallas_call(
        _attn_kernel,
        grid=(batch, N_KV_HEADS),
        in_specs=[pl.BlockSpec((1, GROUP, seq, HEAD_DIM), lambda b, kv: (b, kv, 0, 0)),
                  pl.BlockSpec((1, 1, n_chunks, KEY_CHUNK, HEAD_DIM), lambda b, kv: (b, kv, 0, 0, 0)),
                  pl.BlockSpec((1, 1, V_ROWS, seq), lambda b, kv: (b, kv, 0, 0)),
                  pl.BlockSpec((1, 1, ctx, HEAD_DIM), lambda b, kv: (b, kv, 0, 0)),
                  pl.BlockSpec((1, 1, V_ROWS, ctx), lambda b, kv: (b, kv, 0, 0)),
                  pl.BlockSpec((seq, group_w), lambda b, kv: (b, kv))],
        out_specs=pl.BlockSpec((seq, group_w), lambda b, kv: (b, kv)),
        out_shape=jax.ShapeDtypeStruct((batch * seq, ATTN_WIDTH), _BF16),
        scratch_shapes=[[pltpu.VMEM((n_keys, Q_TILE), _F32)] * 2,
                        [pltpu.VMEM((8, Q_TILE), _F32)] * 2,
                        [pltpu.VMEM((n_keys, Q_TILE), _BF16)] * 2],
        compiler_params=_params(2),
        name="attention",
    )(q, k5, vt, k_ctx, vt_ctx, ga)


def _out_kernel(fix_every, n_tiles, attn_ref, cv_ref, fix_ref, wa_ref, wc_ref, x_ref, gate_ref, gf_ref,
                o_ref):
    i = pl.program_id(0)
    tm = cv_ref.shape[0]
    slab_row = lax.broadcasted_iota(jnp.int32, (HALO_ROWS, 1), 0)
    ends_conv_tile = jnp.logical_and((i % fix_every) == fix_every - 1, i < n_tiles - 1)
    needs_fix = jnp.logical_and(slab_row == HALO_ROWS - 1, ends_conv_tile)
    tail = cv_ref[tm - HALO_ROWS:tm, :].astype(_F32) + jnp.where(needs_fix, fix_ref[0, 0:1, :], 0.0)
    cv = jnp.concatenate([cv_ref[0:tm - HALO_ROWS, :], tail.astype(_BF16)], axis=0)
    y = jnp.dot(attn_ref[...], wa_ref[...], preferred_element_type=_F32)
    y += jnp.dot(cv, wc_ref[...], preferred_element_type=_F32)
    xn = x_ref[...] + gate_ref[0] * y
    o_ref[...] = xn * lax.rsqrt(jnp.mean(xn * xn, axis=-1, keepdims=True) + EPS) * gf_ref[...]


def _out_proj(attn, cv, fix, w_out_bf, x2d, gate, gf, seq, tm):
    m, d = x2d.shape
    tiles_per_seq = seq // tm
    fix_every = (m // tm) // fix.shape[0]
    last_fix = fix.shape[0] - 1
    mix_spec = pl.BlockSpec((tm, ATTN_WIDTH), lambda i: (i, 0))
    return pl.pallas_call(
        functools.partial(_out_kernel, fix_every, m // tm),
        grid=(m // tm,),
        in_specs=[mix_spec, mix_spec,
                  pl.BlockSpec((1, 8, CONV_WIDTH),
                               lambda i: (jnp.minimum(i // fix_every + 1, last_fix), 0, 0)),
                  pl.BlockSpec((ATTN_WIDTH, d), lambda i: (0, 0)),
                  pl.BlockSpec((CONV_WIDTH, d), lambda i: (1, 0)),
                  pl.BlockSpec((tm, d), lambda i: (i, 0)),
                  pl.BlockSpec((1, 1, d), lambda i: (i // tiles_per_seq, 0, 0)),
                  pl.BlockSpec((1, d), lambda i: (0, 0))],
        out_specs=pl.BlockSpec((tm, d), lambda i: (i, 0)),
        out_shape=jax.ShapeDtypeStruct((m, d), _F32),
        compiler_params=_params(1),
        name="out_proj",
    )(attn, cv, fix, w_out_bf, w_out_bf, x2d, gate, gf)


def _rope_tables(seq):
    quarter = HEAD_DIM // 4
    t = np.arange(seq)
    row = (t // GRID_W).astype(np.float64)
    col = (t % GRID_W).astype(np.float64)
    inv = ROPE_THETA ** (-np.arange(quarter, dtype=np.float64) / quarter)
    ang = np.concatenate([row[:, None] * inv, row[:, None] * inv,
                          col[:, None] * inv, col[:, None] * inv], axis=1)
    lane = np.arange(HEAD_DIM) % (2 * quarter)
    sin = np.sin(ang)
    tables = (np.cos(ang), np.where(lane < quarter, -sin, 0.0), np.where(lane >= quarter, sin, 0.0))
    return tuple(jnp.asarray(tab.astype(np.float32)) for tab in tables)


def kernel(x, c, ctx, c_ctx, w_mod, b_mod, norm_g, w_in, q_norm_g, k_norm_g, conv_w, w_out,
           final_norm_g):
    batch, seq, d = x.shape
    ctx_len = ctx.shape[1]
    assert w_mod.shape[0] == 1, "single-layer kernel"
    assert seq % ROW_TILE == 0 and seq % KEY_CHUNK == 0 and seq % Q_TILE == 0

    cond_rows = jnp.concatenate([c, c_ctx[None, :], jnp.zeros((MOD_ROWS - batch - 1, d), _F32)], axis=0)
    mod = _adaln(cond_rows, w_mod[0], b_mod[0][None, :])
    shift = mod[:batch + 1, None, 0 * d:1 * d]
    scale = mod[:batch + 1, None, 1 * d:2 * d]
    gate = mod[:batch, None, 2 * d:3 * d]

    g_in = norm_g[0][None, :]
    q_gain = (q_norm_g[0] * (math.log2(math.e) / math.sqrt(HEAD_DIM)))[None, :]
    k_gain = k_norm_g[0][None, :]

    ctx2d = ctx.reshape(batch * ctx_len, d)
    k_ctx, vt_ctx = _ctx_kv(ctx2d, g_in, scale[batch:], shift[batch:], w_in[0], k_gain, batch, ctx_len)

    x2d = x.reshape(batch * seq, d)
    tables = _rope_tables(seq)
    h, q, k, vt, ga, w_conv_bf = _proj_attn(x2d, g_in, scale[:batch], shift[:batch], w_in[0],
                                            q_gain, k_gain, tables, batch, seq, ROW_TILE)
    cv, fix, w_out_bf = _proj_conv(h, w_conv_bf, conv_w[0], w_out[0], seq, CONV_ROW_TILE)

    attn = _attention(q, k, vt, k_ctx, vt_ctx, ga, batch, seq)
    out = _out_proj(attn, cv, fix, w_out_bf, x2d, gate, final_norm_g[None, :], seq, ROW_TILE)
    return out.reshape(batch, seq, d)
```

```python
import functools
import math

import jax
import jax.numpy as jnp
import numpy as np
from jax import lax
from jax.experimental import pallas as pl
from jax.experimental.pallas import tpu as pltpu

D_MODEL = 2048
CTX_LEN = 256
GRID_W = 64
ATTN_WIDTH = 1024
CONV_WIDTH = 1024
HEAD_DIM = 128
N_HEADS = 8
N_KV_HEADS = 2
GROUP = N_HEADS // N_KV_HEADS
KV_WIDTH = N_KV_HEADS * HEAD_DIM
ROPE_THETA = 10000.0
EPS = 1e-6

LANES = 128
ROW_TILE = 512
CONV_ROW_TILE = 1024
SUB_ROWS = 512
COL_TILE = 512
KEY_CHUNK = 2048
Q_TILE = 256
V_ROWS = HEAD_DIM + 16
HALO_ROWS = 16
MOD_ROWS = 8
MOD_ROW_TILE = 256
VMEM_LIMIT = 56 * 1024 * 1024
assert 2 * KV_WIDTH == COL_TILE and ATTN_WIDTH % COL_TILE == 0 and CONV_WIDTH % COL_TILE == 0

_F32 = jnp.float32
_BF16 = jnp.bfloat16


def _silu(v):
    half = 0.5 * v
    return half + half * jnp.tanh(half)


def _params(n_axes, flags=None):
    return pltpu.CompilerParams(dimension_semantics=("arbitrary",) * n_axes,
                                vmem_limit_bytes=VMEM_LIMIT, flags=flags)


def _adaln_kernel(cond_ref, w_ref, b_ref, o_ref):
    @pl.when(pl.program_id(0) == 0)
    def _():
        o_ref[...] = jnp.broadcast_to(b_ref[...], o_ref.shape)

    s = _silu(cond_ref[0]).astype(_BF16)
    o_ref[...] += jnp.dot(s, w_ref[...].astype(_BF16), preferred_element_type=_F32)


def _adaln(cond_rows, w_mod, b_mod):
    d, n = w_mod.shape
    bands = d // MOD_ROW_TILE
    cond_bands = cond_rows.reshape(MOD_ROWS, bands, MOD_ROW_TILE).transpose(1, 0, 2)
    return pl.pallas_call(
        _adaln_kernel,
        grid=(bands,),
        in_specs=[pl.BlockSpec((1, MOD_ROWS, MOD_ROW_TILE), lambda k: (k, 0, 0)),
                  pl.BlockSpec((MOD_ROW_TILE, n), lambda k: (k, 0)),
                  pl.BlockSpec((1, n), lambda k: (0, 0))],
        out_specs=pl.BlockSpec((MOD_ROWS, n), lambda k: (0, 0)),
        out_shape=jax.ShapeDtypeStruct((MOD_ROWS, n), _F32),
        compiler_params=_params(1),
        name="adaln",
    )(cond_bands, w_mod, b_mod)


def _modulated_norm(x_ref, gin_ref, scale_ref, shift_ref, rows=slice(None)):
    xf = x_ref[rows, :]
    y = xf * lax.rsqrt(jnp.mean(xf * xf, axis=-1, keepdims=True) + EPS) * gin_ref[...]
    return (y * (1.0 + scale_ref[0]) + shift_ref[0]).astype(_BF16)


def _norm_rope(xh, g, rope):
    y = xh * lax.rsqrt(jnp.mean(xh * xh, axis=-1, keepdims=True) + EPS) * g
    if rope is None:
        return y
    cos, sin_lo, sin_hi = rope
    return (y * cos + pltpu.roll(y, 3 * HEAD_DIM // 4, axis=1) * sin_lo
            + pltpu.roll(y, HEAD_DIM // 4, axis=1) * sin_hi)


def _store_kv(acc, g, rope, k_ref, vt_ref, rows=slice(None)):
    for hh in range(N_KV_HEADS):
        xh = acc[:, hh * HEAD_DIM:(hh + 1) * HEAD_DIM]
        k_ref[0, hh, rows, :] = _norm_rope(xh, g, rope).astype(_BF16)
        vh = acc[:, KV_WIDTH + hh * HEAD_DIM:KV_WIDTH + (hh + 1) * HEAD_DIM]
        vt_ref[0, hh, 0:HEAD_DIM, rows] = vh.T.astype(_BF16)
        vt_ref[0, hh, HEAD_DIM:V_ROWS, rows] = jnp.ones((V_ROWS - HEAD_DIM, vh.shape[0]), _BF16)


def _ctx_kv_kernel(x_ref, gin_ref, scale_ref, shift_ref, w_ref, g_ref, k_ref, vt_ref):
    h = _modulated_norm(x_ref, gin_ref, scale_ref, shift_ref)
    acc = jnp.dot(h, w_ref[...].astype(_BF16), preferred_element_type=_F32)
    _store_kv(acc, g_ref[...], None, k_ref, vt_ref)


def _proj_attn_kernel(x_ref, gin_ref, scale_ref, shift_ref, w_ref, qg_ref, kg_ref,
                      cos_ref, slo_ref, shi_ref, wside_ref, h_ref, q_ref, k_ref, vt_ref, ga_ref,
                      wside_bf_ref):
    wside_bf_ref[...] = wside_ref[...].astype(_BF16)
    half = x_ref.shape[0] // 2
    halves = [slice(r0, r0 + half) for r0 in range(0, x_ref.shape[0], half)]
    hs = []
    for rows in halves:
        h = _modulated_norm(x_ref, gin_ref, scale_ref, shift_ref, rows)
        h_ref[rows, :] = h
        hs.append(h)
    ropes = [(cos_ref[rows, :], slo_ref[rows, :], shi_ref[rows, :]) for rows in halves]
    heads_per_dot = COL_TILE // HEAD_DIM

    def project(col0):
        w = w_ref[:, col0:col0 + COL_TILE].astype(_BF16)
        return [jnp.dot(h, w, preferred_element_type=_F32) for h in hs]

    for blk in range(ATTN_WIDTH // COL_TILE):
        for rows, rope, acc in zip(halves, ropes, project(blk * COL_TILE)):
            for hh in range(heads_per_dot):
                xh = acc[:, hh * HEAD_DIM:(hh + 1) * HEAD_DIM]
                q_ref[0, blk * heads_per_dot + hh, rows, :] = _norm_rope(xh, qg_ref[...], rope).astype(_BF16)
    for rows, rope, acc in zip(halves, ropes, project(ATTN_WIDTH)):
        _store_kv(acc, kg_ref[...], rope, k_ref, vt_ref, rows)
    gate_col = ATTN_WIDTH + 2 * KV_WIDTH
    for blk in range(ATTN_WIDTH // COL_TILE):
        for rows, acc in zip(halves, project(gate_col + blk * COL_TILE)):
            ga_ref[rows, blk * COL_TILE:(blk + 1) * COL_TILE] = _silu(acc).astype(_BF16)


def _proj_conv_kernel(tiles_per_seq, h_ref, wb_ref, wcg_ref, wh_ref, wgc_ref, cw_ref, wside_ref,
                      cv_ref, fix_ref, wside_bf_ref, u_last, e_last):
    i = pl.program_id(1)
    tm = h_ref.shape[0]
    wside_bf_ref[...] = wside_ref[...].astype(_BF16)

    @pl.when(i == 0)
    def _():
        u_last[...] = jnp.zeros(u_last.shape, _F32)
        e_last[...] = jnp.zeros(e_last.shape, _F32)

    def project(w_ref):
        return jnp.concatenate(
            [jnp.dot(h_ref[r0:r0 + SUB_ROWS, :], w_ref[...], preferred_element_type=_F32)
             for r0 in range(0, tm, SUB_ROWS)], axis=0)

    u = project(wcg_ref) * project(wh_ref)
    e = project(wb_ref) * _silu(project(wgc_ref))
    w0, w1, w2 = cw_ref[0:1, :], cw_ref[1:2, :], cw_ref[2:3, :]
    seq_start = (i % tiles_per_seq) == 0
    prev_row = jnp.where(seq_start, 0.0, u_last[...])
    slab_row = lax.broadcasted_iota(jnp.int32, (8, 1), 0)
    down, up = pltpu.roll(u, 1, axis=0), pltpu.roll(u, tm - 1, axis=0)
    u_before = jnp.concatenate([jnp.where(slab_row == 0, prev_row, down[0:8]), down[8:]], axis=0)
    u_after = jnp.concatenate([up[:tm - 8], jnp.where(slab_row == 7, 0.0, up[tm - 8:])], axis=0)
    cv_ref[...] = (e * (u_before * w0 + u * w1 + u_after * w2)).astype(_BF16)
    fix = jnp.where(seq_start, 0.0, e_last[...] * w2 * u[0:1])
    fix_ref[0] = jnp.where(slab_row == 0, fix, 0.0)
    u_last[...] = u[tm - 1:tm]
    e_last[...] = e[tm - 1:tm]


def _rope_specs(tm, seq):
    tiles = seq // tm
    return [pl.BlockSpec((tm, HEAD_DIM), lambda i, *_: (i % tiles, 0))] * 3


def _proj_attn(x2d, gin, scale, shift, w_in, q_gain, k_gain, tables, batch, seq, tm):
    m, d = x2d.shape
    tiles = seq // tm
    steps = m // tm
    n_cols = 2 * ATTN_WIDTH + 2 * KV_WIDTH
    side_cols = (w_in.shape[1] - n_cols) // steps
    assert side_cols % LANES == 0 and n_cols % side_cols == 0
    side_first = n_cols // side_cols
    mod_spec = pl.BlockSpec((1, 1, d), lambda i: (i // tiles, 0, 0))
    gain_spec = pl.BlockSpec((1, HEAD_DIM), lambda i: (0, 0))
    return pl.pallas_call(
        _proj_attn_kernel,
        grid=(steps,),
        in_specs=[pl.BlockSpec((tm, d), lambda i: (i, 0)),
                  pl.BlockSpec((1, d), lambda i: (0, 0)),
                  mod_spec, mod_spec,
                  pl.BlockSpec((d, n_cols), lambda i: (0, 0)),
                  gain_spec, gain_spec] + _rope_specs(tm, seq)
                 + [pl.BlockSpec((d, side_cols), lambda i: (0, side_first + i))],
        out_specs=[pl.BlockSpec((tm, d), lambda i: (i, 0)),
                   pl.BlockSpec((1, N_HEADS, tm, HEAD_DIM), lambda i: (i // tiles, 0, i % tiles, 0)),
                   pl.BlockSpec((1, N_KV_HEADS, tm, HEAD_DIM), lambda i: (i // tiles, 0, i % tiles, 0)),
                   pl.BlockSpec((1, N_KV_HEADS, V_ROWS, tm), lambda i: (i // tiles, 0, 0, i % tiles)),
                   pl.BlockSpec((tm, ATTN_WIDTH), lambda i: (i, 0)),
                   pl.BlockSpec((d, side_cols), lambda i: (0, i))],
        out_shape=[jax.ShapeDtypeStruct((m, d), _BF16),
                   jax.ShapeDtypeStruct((batch, N_HEADS, seq, HEAD_DIM), _BF16),
                   jax.ShapeDtypeStruct((batch, N_KV_HEADS, seq, HEAD_DIM), _BF16),
                   jax.ShapeDtypeStruct((batch, N_KV_HEADS, V_ROWS, seq), _BF16),
                   jax.ShapeDtypeStruct((m, ATTN_WIDTH), _BF16),
                   jax.ShapeDtypeStruct((d, side_cols * steps), _BF16)],
        compiler_params=_params(1),
        name="proj_attn",
    )(x2d, gin, scale, shift, w_in, q_gain, k_gain, *tables, w_in)


def _ctx_kv(ctx2d, gin, scale, shift, w_in, k_gain, batch, ctx_len):
    d = ctx2d.shape[1]
    kv_col_block = ATTN_WIDTH // COL_TILE
    mod_spec = pl.BlockSpec((1, 1, d), lambda i: (0, 0, 0))
    return pl.pallas_call(
        _ctx_kv_kernel,
        grid=(batch,),
        in_specs=[pl.BlockSpec((ctx_len, d), lambda i: (i, 0)),
                  pl.BlockSpec((1, d), lambda i: (0, 0)),
                  mod_spec, mod_spec,
                  pl.BlockSpec((d, 2 * KV_WIDTH), lambda i: (0, kv_col_block)),
                  pl.BlockSpec((1, HEAD_DIM), lambda i: (0, 0))],
        out_specs=[pl.BlockSpec((1, N_KV_HEADS, ctx_len, HEAD_DIM), lambda i: (i, 0, 0, 0)),
                   pl.BlockSpec((1, N_KV_HEADS, V_ROWS, ctx_len), lambda i: (i, 0, 0, 0))],
        out_shape=[jax.ShapeDtypeStruct((batch, N_KV_HEADS, ctx_len, HEAD_DIM), _BF16),
                   jax.ShapeDtypeStruct((batch, N_KV_HEADS, V_ROWS, ctx_len), _BF16)],
        compiler_params=_params(1),
        name="ctx_kv",
    )(ctx2d, gin, scale, shift, w_in, k_gain)


def _proj_conv(h, w_conv_bf, conv_w, w_out, seq, tm):
    m, d = h.shape
    step = CONV_WIDTH // COL_TILE
    row_tiles = m // tm
    side_rows = w_out.shape[0] // (step * row_tiles)
    assert side_rows % HALO_ROWS == 0 and seq % tm == 0

    def w_spec(group):
        return pl.BlockSpec((d, COL_TILE), lambda j, i: (0, group * step + j))

    side_spec = pl.BlockSpec((side_rows, w_out.shape[1]), lambda j, i: (j * row_tiles + i, 0))
    edge_row = pltpu.VMEM((1, COL_TILE), _F32)
    return pl.pallas_call(
        functools.partial(_proj_conv_kernel, seq // tm),
        grid=(step, row_tiles),
        in_specs=[pl.BlockSpec((tm, d), lambda j, i: (i, 0)),
                  w_spec(0), w_spec(1), w_spec(2), w_spec(3),
                  pl.BlockSpec((3, COL_TILE), lambda j, i: (0, j)), side_spec],
        out_specs=[pl.BlockSpec((tm, COL_TILE), lambda j, i: (i, j)),
                   pl.BlockSpec((1, 8, COL_TILE), lambda j, i: (i, 0, j)),
                   side_spec],
        out_shape=[jax.ShapeDtypeStruct((m, CONV_WIDTH), _BF16),
                   jax.ShapeDtypeStruct((row_tiles, 8, CONV_WIDTH), _F32),
                   jax.ShapeDtypeStruct(w_out.shape, _BF16)],
        scratch_shapes=[edge_row, edge_row],
        compiler_params=_params(2),
        name="proj_conv",
    )(h, w_conv_bf, w_conv_bf, w_conv_bf, w_conv_bf, conv_w, w_out)


def _attn_kernel(q_ref, k_ref, vt_ref, kc_ref, vtc_ref, ga_ref, o_ref, s_ring, m_ring, p_ring):
    n_chunks, kc = k_ref.shape[2], k_ref.shape[3]
    n_lat, ctx = n_chunks * kc, kc_ref.shape[2]
    tq = s_ring[0].shape[1]
    n_tiles = q_ref.shape[2] // tq
    nt = (((1,), (1,)), ((), ()))

    def fold(v):
        return v.reshape(v.shape[0] // 8, 8, v.shape[1])

    def rows(t):
        return pl.ds(pl.multiple_of(t * tq, tq), tq)

    def key_block(c):
        return (k_ref[0, 0, c], pl.ds(c * kc, kc)) if c < n_chunks else (kc_ref[0, 0], pl.ds(n_lat, ctx))

    def scores(t, g, c):
        keys, span = key_block(c)
        s = lax.dot_general(keys, q_ref[0, g, rows(t), :], nt, preferred_element_type=_F32)
        s_ring[g % 2][span, :] = s
        chunk_max = fold(s).max(axis=0)
        m_buf = m_ring[g % 2]
        m_buf[...] = chunk_max if c == 0 else jnp.maximum(m_buf[...], chunk_max)

    def probs(g, c, m):
        _, span = key_block(c)
        p_ring[g % 2][span, :] = jnp.exp2(s_ring[g % 2][span, :] - m).astype(_BF16)

    def values(t, g):
        p_buf = p_ring[g % 2]
        acc = jnp.dot(vt_ref[0, 0], p_buf[0:n_lat, :], preferred_element_type=_F32)
        acc += jnp.dot(vtc_ref[0, 0], p_buf[n_lat:n_lat + ctx, :], preferred_element_type=_F32)
        o = (acc[0:HEAD_DIM] * (1.0 / acc[HEAD_DIM:HEAD_DIM + 1])).T
        cols = slice(g * HEAD_DIM, (g + 1) * HEAD_DIM)
        o_ref[rows(t), cols] = (o * ga_ref[rows(t), cols].astype(_F32)).astype(_BF16)

    def phase(score_item, prob_head, value_item):
        if prob_head is not None:
            m = m_ring[prob_head % 2][...].max(axis=0, keepdims=True)
        for c in range(n_chunks + 1):
            if score_item is not None:
                scores(*score_item, c)
            if prob_head is not None:
                probs(prob_head, c, m)
        if value_item is not None:
            values(*value_item)

    assert GROUP == 4
    phase((0, 0), None, None)
    phase((0, 1), 0, None)

    def tile(t, carry):
        nxt = jnp.minimum(t + 1, n_tiles - 1)
        phase((t, 2), 1, (t, 0))
        phase((t, 3), 2, (t, 1))
        phase((nxt, 0), 3, (t, 2))
        phase((nxt, 1), 0, (t, 3))
        return carry

    lax.fori_loop(0, n_tiles, tile, 0, unroll=4)


def _attention(q, k, vt, k_ctx, vt_ctx, ga, batch, seq):
    n_chunks = seq // KEY_CHUNK
    ctx = k_ctx.shape[2]
    k5 = k.reshape(batch, N_KV_HEADS, n_chunks, KEY_CHUNK, HEAD_DIM)
    group_w = GROUP * HEAD_DIM
    n_keys = seq + ctx
    return pl.pallas_call(
        _attn_kernel,
        grid=(batch, N_KV_HEADS),
        in_specs=[pl.BlockSpec((1, GROUP, seq, HEAD_DIM), lambda b, kv: (b, kv, 0, 0)),
                  pl.BlockSpec((1, 1, n_chunks, KEY_CHUNK, HEAD_DIM), lambda b, kv: (b, kv, 0, 0, 0)),
                  pl.BlockSpec((1, 1, V_ROWS, seq), lambda b, kv: (b, kv, 0, 0)),
                  pl.BlockSpec((1, 1, ctx, HEAD_DIM), lambda b, kv: (b, kv, 0, 0)),
                  pl.BlockSpec((1, 1, V_ROWS, ctx), lambda b, kv: (b, kv, 0, 0)),
                  pl.BlockSpec((seq, group_w), lambda b, kv: (b, kv))],
        out_specs=pl.BlockSpec((seq, group_w), lambda b, kv: (b, kv)),
        out_shape=jax.ShapeDtypeStruct((batch * seq, ATTN_WIDTH), _BF16),
        scratch_shapes=[[pltpu.VMEM((n_keys, Q_TILE), _F32)] * 2,
                        [pltpu.VMEM((8, Q_TILE), _F32)] * 2,
                        [pltpu.VMEM((n_keys, Q_TILE), _BF16)] * 2],
        compiler_params=_params(2),
        name="attention",
    )(q, k5, vt, k_ctx, vt_ctx, ga)


def _out_kernel(fix_every, n_tiles, attn_ref, cv_ref, fix_ref, wa_ref, wc_ref, x_ref, gate_ref, gf_ref,
                o_ref):
    i = pl.program_id(0)
    tm = cv_ref.shape[0]
    slab_row = lax.broadcasted_iota(jnp.int32, (HALO_ROWS, 1), 0)
    ends_conv_tile = jnp.logical_and((i % fix_every) == fix_every - 1, i < n_tiles - 1)
    needs_fix = jnp.logical_and(slab_row == HALO_ROWS - 1, ends_conv_tile)
    tail = cv_ref[tm - HALO_ROWS:tm, :].astype(_F32) + jnp.where(needs_fix, fix_ref[0, 0:1, :], 0.0)
    cv = jnp.concatenate([cv_ref[0:tm - HALO_ROWS, :], tail.astype(_BF16)], axis=0)
    y = jnp.dot(attn_ref[...], wa_ref[...], preferred_element_type=_F32)
    y += jnp.dot(cv, wc_ref[...], preferred_element_type=_F32)
    xn = x_ref[...] + gate_ref[0] * y
    o_ref[...] = xn * lax.rsqrt(jnp.mean(xn * xn, axis=-1, keepdims=True) + EPS) * gf_ref[...]


def _out_proj(attn, cv, fix, w_out_bf, x2d, gate, gf, seq, tm):
    m, d = x2d.shape
    tiles_per_seq = seq // tm
    fix_every = (m // tm) // fix.shape[0]
    last_fix = fix.shape[0] - 1
    mix_spec = pl.BlockSpec((tm, ATTN_WIDTH), lambda i: (i, 0))
    return pl.pallas_call(
        functools.partial(_out_kernel, fix_every, m // tm),
        grid=(m // tm,),
        in_specs=[mix_spec, mix_spec,
                  pl.BlockSpec((1, 8, CONV_WIDTH),
                               lambda i: (jnp.minimum(i // fix_every + 1, last_fix), 0, 0)),
                  pl.BlockSpec((ATTN_WIDTH, d), lambda i: (0, 0)),
                  pl.BlockSpec((CONV_WIDTH, d), lambda i: (1, 0)),
                  pl.BlockSpec((tm, d), lambda i: (i, 0)),
                  pl.BlockSpec((1, 1, d), lambda i: (i // tiles_per_seq, 0, 0)),
                  pl.BlockSpec((1, d), lambda i: (0, 0))],
        out_specs=pl.BlockSpec((tm, d), lambda i: (i, 0)),
        out_shape=jax.ShapeDtypeStruct((m, d), _F32),
        compiler_params=_params(1),
        name="out_proj",
    )(attn, cv, fix, w_out_bf, w_out_bf, x2d, gate, gf)


def _rope_tables(seq):
    quarter = HEAD_DIM // 4
    t = np.arange(seq)
    row = (t // GRID_W).astype(np.float64)
    col = (t % GRID_W).astype(np.float64)
    inv = ROPE_THETA ** (-np.arange(quarter, dtype=np.float64) / quarter)
    ang = np.concatenate([row[:, None] * inv, row[:, None] * inv,
                          col[:, None] * inv, col[:, None] * inv], axis=1)
    lane = np.arange(HEAD_DIM) % (2 * quarter)
    sin = np.sin(ang)
    tables = (np.cos(ang), np.where(lane < quarter, -sin, 0.0), np.where(lane >= quarter, sin, 0.0))
    return tuple(jnp.asarray(tab.astype(np.float32)) for tab in tables)


def kernel(x, c, ctx, c_ctx, w_mod, b_mod, norm_g, w_in, q_norm_g, k_norm_g, conv_w, w_out,
           final_norm_g):
    batch, seq, d = x.shape
    ctx_len = ctx.shape[1]
    assert w_mod.shape[0] == 1, "single-layer kernel"
    assert seq % ROW_TILE == 0 and seq % KEY_CHUNK == 0 and seq % Q_TILE == 0

    cond_rows = jnp.concatenate([c, c_ctx[None, :], jnp.zeros((MOD_ROWS - batch - 1, d), _F32)], axis=0)
    mod = _adaln(cond_rows, w_mod[0], b_mod[0][None, :])
    shift = mod[:batch + 1, None, 0 * d:1 * d]
    scale = mod[:batch + 1, None, 1 * d:2 * d]
    gate = mod[:batch, None, 2 * d:3 * d]

    g_in = norm_g[0][None, :]
    q_gain = (q_norm_g[0] * (math.log2(math.e) / math.sqrt(HEAD_DIM)))[None, :]
    k_gain = k_norm_g[0][None, :]

    ctx2d = ctx.reshape(batch * ctx_len, d)
    k_ctx, vt_ctx = _ctx_kv(ctx2d, g_in, scale[batch:], shift[batch:], w_in[0], k_gain, batch, ctx_len)

    x2d = x.reshape(batch * seq, d)
    tables = _rope_tables(seq)
    h, q, k, vt, ga, w_conv_bf = _proj_attn(x2d, g_in, scale[:batch], shift[:batch], w_in[0],
                                            q_gain, k_gain, tables, batch, seq, ROW_TILE)
    cv, fix, w_out_bf = _proj_conv(h, w_conv_bf, conv_w[0], w_out[0], seq, CONV_ROW_TILE)

    attn = _attention(q, k, vt, k_ctx, vt_ctx, ga, batch, seq)
    out = _out_proj(attn, cv, fix, w_out_bf, x2d, gate, final_norm_g[None, :], seq, ROW_TILE)
    return out.reshape(batch, seq, d)
```

```python
import functools
import math

import jax
import jax.numpy as jnp
import numpy as np
from jax import lax
from jax.experimental import pallas as pl
from jax.experimental.pallas import tpu as pltpu

D_MODEL = 2048
CTX_LEN = 256
GRID_W = 64
ATTN_WIDTH = 1024
CONV_WIDTH = 1024
HEAD_DIM = 128
N_HEADS = 8
N_KV_HEADS = 2
GROUP = N_HEADS // N_KV_HEADS
KV_WIDTH = N_KV_HEADS * HEAD_DIM
ROPE_THETA = 10000.0
EPS = 1e-6

LANES = 128
ROW_TILE = 512
CONV_ROW_TILE = 1024
SUB_ROWS = 512
COL_TILE = 512
KEY_CHUNK = 2048
Q_TILE = 256
V_ROWS = HEAD_DIM + 16
HALO_ROWS = 16
MOD_ROWS = 8
MOD_ROW_TILE = 256
VMEM_LIMIT = 56 * 1024 * 1024
assert 2 * KV_WIDTH == COL_TILE and ATTN_WIDTH % COL_TILE == 0 and CONV_WIDTH % COL_TILE == 0

_F32 = jnp.float32
_BF16 = jnp.bfloat16


def _silu(v):
    half = 0.5 * v
    return half + half * jnp.tanh(half)


def _params(n_axes, flags=None):
    return pltpu.CompilerParams(dimension_semantics=("arbitrary",) * n_axes,
                                vmem_limit_bytes=VMEM_LIMIT, flags=flags)


def _adaln_kernel(cond_ref, w_ref, b_ref, o_ref):
    @pl.when(pl.program_id(0) == 0)
    def _():
        o_ref[...] = jnp.broadcast_to(b_ref[...], o_ref.shape)

    s = _silu(cond_ref[0]).astype(_BF16)
    o_ref[...] += jnp.dot(s, w_ref[...].astype(_BF16), preferred_element_type=_F32)


def _adaln(cond_rows, w_mod, b_mod):
    d, n = w_mod.shape
    bands = d // MOD_ROW_TILE
    cond_bands = cond_rows.reshape(MOD_ROWS, bands, MOD_ROW_TILE).transpose(1, 0, 2)
    return pl.pallas_call(
        _adaln_kernel,
        grid=(bands,),
        in_specs=[pl.BlockSpec((1, MOD_ROWS, MOD_ROW_TILE), lambda k: (k, 0, 0)),
                  pl.BlockSpec((MOD_ROW_TILE, n), lambda k: (k, 0)),
                  pl.BlockSpec((1, n), lambda k: (0, 0))],
        out_specs=pl.BlockSpec((MOD_ROWS, n), lambda k: (0, 0)),
        out_shape=jax.ShapeDtypeStruct((MOD_ROWS, n), _F32),
        compiler_params=_params(1),
        name="adaln",
    )(cond_bands, w_mod, b_mod)


def _modulated_norm(x_ref, gin_ref, scale_ref, shift_ref, rows=slice(None)):
    xf = x_ref[rows, :]
    y = xf * lax.rsqrt(jnp.mean(xf * xf, axis=-1, keepdims=True) + EPS) * gin_ref[...]
    return (y * (1.0 + scale_ref[0]) + shift_ref[0]).astype(_BF16)


def _norm_rope(xh, g, rope):
    y = xh * lax.rsqrt(jnp.mean(xh * xh, axis=-1, keepdims=True) + EPS) * g
    if rope is None:
        return y
    cos, sin_lo, sin_hi = rope
    return (y * cos + pltpu.roll(y, 3 * HEAD_DIM // 4, axis=1) * sin_lo
            + pltpu.roll(y, HEAD_DIM // 4, axis=1) * sin_hi)


def _store_kv(acc, g, rope, k_ref, vt_ref, rows=slice(None)):
    for hh in range(N_KV_HEADS):
        xh = acc[:, hh * HEAD_DIM:(hh + 1) * HEAD_DIM]
        k_ref[0, hh, rows, :] = _norm_rope(xh, g, rope).astype(_BF16)
        vh = acc[:, KV_WIDTH + hh * HEAD_DIM:KV_WIDTH + (hh + 1) * HEAD_DIM]
        vt_ref[0, hh, 0:HEAD_DIM, rows] = vh.T.astype(_BF16)
        vt_ref[0, hh, HEAD_DIM:V_ROWS, rows] = jnp.ones((V_ROWS - HEAD_DIM, vh.shape[0]), _BF16)


def _ctx_kv_kernel(x_ref, gin_ref, scale_ref, shift_ref, w_ref, g_ref, k_ref, vt_ref):
    h = _modulated_norm(x_ref, gin_ref, scale_ref, shift_ref)
    acc = jnp.dot(h, w_ref[...].astype(_BF16), preferred_element_type=_F32)
    _store_kv(acc, g_ref[...], None, k_ref, vt_ref)


def _proj_attn_kernel(x_ref, gin_ref, scale_ref, shift_ref, w_ref, qg_ref, kg_ref,
                      cos_ref, slo_ref, shi_ref, wside_ref, h_ref, q_ref, k_ref, vt_ref, ga_ref,
                      wside_bf_ref):
    wside_bf_ref[...] = wside_ref[...].astype(_BF16)
    half = x_ref.shape[0] // 2
    halves = [slice(r0, r0 + half) for r0 in range(0, x_ref.shape[0], half)]
    hs = []
    for rows in halves:
        h = _modulated_norm(x_ref, gin_ref, scale_ref, shift_ref, rows)
        h_ref[rows, :] = h
        hs.append(h)
    ropes = [(cos_ref[rows, :], slo_ref[rows, :], shi_ref[rows, :]) for rows in halves]
    heads_per_dot = COL_TILE // HEAD_DIM

    def project(col0):
        w = w_ref[:, col0:col0 + COL_TILE].astype(_BF16)
        return [jnp.dot(h, w, preferred_element_type=_F32) for h in hs]

    for blk in range(ATTN_WIDTH // COL_TILE):
        for rows, rope, acc in zip(halves, ropes, project(blk * COL_TILE)):
            for hh in range(heads_per_dot):
                xh = acc[:, hh * HEAD_DIM:(hh + 1) * HEAD_DIM]
                q_ref[0, blk * heads_per_dot + hh, rows, :] = _norm_rope(xh, qg_ref[...], rope).astype(_BF16)
    for rows, rope, acc in zip(halves, ropes, project(ATTN_WIDTH)):
        _store_kv(acc, kg_ref[...], rope, k_ref, vt_ref, rows)
    gate_col = ATTN_WIDTH + 2 * KV_WIDTH
    for blk in range(ATTN_WIDTH // COL_TILE):
        for rows, acc in zip(halves, project(gate_col + blk * COL_TILE)):
            ga_ref[rows, blk * COL_TILE:(blk + 1) * COL_TILE] = _silu(acc).astype(_BF16)


def _proj_conv_kernel(tiles_per_seq, h_ref, wb_ref, wcg_ref, wh_ref, wgc_ref, cw_ref, wside_ref,
                      cv_ref, fix_ref, wside_bf_ref, u_last, e_last):
    i = pl.program_id(1)
    tm = h_ref.shape[0]
    wside_bf_ref[...] = wside_ref[...].astype(_BF16)

    @pl.when(i == 0)
    def _():
        u_last[...] = jnp.zeros(u_last.shape, _F32)
        e_last[...] = jnp.zeros(e_last.shape, _F32)

    def project(w_ref):
        return jnp.concatenate(
            [jnp.dot(h_ref[r0:r0 + SUB_ROWS, :], w_ref[...], preferred_element_type=_F32)
             for r0 in range(0, tm, SUB_ROWS)], axis=0)

    u = project(wcg_ref) * project(wh_ref)
    e = project(wb_ref) * _silu(project(wgc_ref))
    w0, w1, w2 = cw_ref[0:1, :], cw_ref[1:2, :], cw_ref[2:3, :]
    seq_start = (i % tiles_per_seq) == 0
    prev_row = jnp.where(seq_start, 0.0, u_last[...])
    slab_row = lax.broadcasted_iota(jnp.int32, (8, 1), 0)
    down, up = pltpu.roll(u, 1, axis=0), pltpu.roll(u, tm - 1, axis=0)
    u_before = jnp.concatenate([jnp.where(slab_row == 0, prev_row, down[0:8]), down[8:]], axis=0)
    u_after = jnp.concatenate([up[:tm - 8], jnp.where(slab_row == 7, 0.0, up[tm - 8:])], axis=0)
    cv_ref[...] = (e * (u_before * w0 + u * w1 + u_after * w2)).astype(_BF16)
    fix = jnp.where(seq_start, 0.0, e_last[...] * w2 * u[0:1])
    fix_ref[0] = jnp.where(slab_row == 0, fix, 0.0)
    u_last[...] = u[tm - 1:tm]
    e_last[...] = e[tm - 1:tm]


def _rope_specs(tm, seq):
    tiles = seq // tm
    return [pl.BlockSpec((tm, HEAD_DIM), lambda i, *_: (i % tiles, 0))] * 3


def _proj_attn(x2d, gin, scale, shift, w_in, q_gain, k_gain, tables, batch, seq, tm):
    m, d = x2d.shape
    tiles = seq // tm
    steps = m // tm
    n_cols = 2 * ATTN_WIDTH + 2 * KV_WIDTH
    side_cols = (w_in.shape[1] - n_cols) // steps
    assert side_cols % LANES == 0 and n_cols % side_cols == 0
    side_first = n_cols // side_cols
    mod_spec = pl.BlockSpec((1, 1, d), lambda i: (i // tiles, 0, 0))
    gain_spec = pl.BlockSpec((1, HEAD_DIM), lambda i: (0, 0))
    return pl.pallas_call(
        _proj_attn_kernel,
        grid=(steps,),
        in_specs=[pl.BlockSpec((tm, d), lambda i: (i, 0)),
                  pl.BlockSpec((1, d), lambda i: (0, 0)),
                  mod_spec, mod_spec,
                  pl.BlockSpec((d, n_cols), lambda i: (0, 0)),
                  gain_spec, gain_spec] + _rope_specs(tm, seq)
                 + [pl.BlockSpec((d, side_cols), lambda i: (0, side_first + i))],
        out_specs=[pl.BlockSpec((tm, d), lambda i: (i, 0)),
                   pl.BlockSpec((1, N_HEADS, tm, HEAD_DIM), lambda i: (i // tiles, 0, i % tiles, 0)),
                   pl.BlockSpec((1, N_KV_HEADS, tm, HEAD_DIM), lambda i: (i // tiles, 0, i % tiles, 0)),
                   pl.BlockSpec((1, N_KV_HEADS, V_ROWS, tm), lambda i: (i // tiles, 0, 0, i % tiles)),
                   pl.BlockSpec((tm, ATTN_WIDTH), lambda i: (i, 0)),
                   pl.BlockSpec((d, side_cols), lambda i: (0, i))],
        out_shape=[jax.ShapeDtypeStruct((m, d), _BF16),
                   jax.ShapeDtypeStruct((batch, N_HEADS, seq, HEAD_DIM), _BF16),
                   jax.ShapeDtypeStruct((batch, N_KV_HEADS, seq, HEAD_DIM), _BF16),
                   jax.ShapeDtypeStruct((batch, N_KV_HEADS, V_ROWS, seq), _BF16),
                   jax.ShapeDtypeStruct((m, ATTN_WIDTH), _BF16),
                   jax.ShapeDtypeStruct((d, side_cols * steps), _BF16)],
        compiler_params=_params(1),
        name="proj_attn",
    )(x2d, gin, scale, shift, w_in, q_gain, k_gain, *tables, w_in)


def _ctx_kv(ctx2d, gin, scale, shift, w_in, k_gain, batch, ctx_len):
    d = ctx2d.shape[1]
    kv_col_block = ATTN_WIDTH // COL_TILE
    mod_spec = pl.BlockSpec((1, 1, d), lambda i: (0, 0, 0))
    return pl.pallas_call(
        _ctx_kv_kernel,
        grid=(batch,),
        in_specs=[pl.BlockSpec((ctx_len, d), lambda i: (i, 0)),
                  pl.BlockSpec((1, d), lambda i: (0, 0)),
                  mod_spec, mod_spec,
                  pl.BlockSpec((d, 2 * KV_WIDTH), lambda i: (0, kv_col_block)),
                  pl.BlockSpec((1, HEAD_DIM), lambda i: (0, 0))],
        out_specs=[pl.BlockSpec((1, N_KV_HEADS, ctx_len, HEAD_DIM), lambda i: (i, 0, 0, 0)),
                   pl.BlockSpec((1, N_KV_HEADS, V_ROWS, ctx_len), lambda i: (i, 0, 0, 0))],
        out_shape=[jax.ShapeDtypeStruct((batch, N_KV_HEADS, ctx_len, HEAD_DIM), _BF16),
                   jax.ShapeDtypeStruct((batch, N_KV_HEADS, V_ROWS, ctx_len), _BF16)],
        compiler_params=_params(1),
        name="ctx_kv",
    )(ctx2d, gin, scale, shift, w_in, k_gain)


def _proj_conv(h, w_conv_bf, conv_w, w_out, seq, tm):
    m, d = h.shape
    step = CONV_WIDTH // COL_TILE
    row_tiles = m // tm
    side_rows = w_out.shape[0] // (step * row_tiles)
    assert side_rows % HALO_ROWS == 0 and seq % tm == 0

    def w_spec(group):
        return pl.BlockSpec((d, COL_TILE), lambda j, i: (0, group * step + j))

    side_spec = pl.BlockSpec((side_rows, w_out.shape[1]), lambda j, i: (j * row_tiles + i, 0))
    edge_row = pltpu.VMEM((1, COL_TILE), _F32)
    return pl.pallas_call(
        functools.partial(_proj_conv_kernel, seq // tm),
        grid=(step, row_tiles),
        in_specs=[pl.BlockSpec((tm, d), lambda j, i: (i, 0)),
                  w_spec(0), w_spec(1), w_spec(2), w_spec(3),
                  pl.BlockSpec((3, COL_TILE), lambda j, i: (0, j)), side_spec],
        out_specs=[pl.BlockSpec((tm, COL_TILE), lambda j, i: (i, j)),
                   pl.BlockSpec((1, 8, COL_TILE), lambda j, i: (i, 0, j)),
                   side_spec],
        out_shape=[jax.ShapeDtypeStruct((m, CONV_WIDTH), _BF16),
                   jax.ShapeDtypeStruct((row_tiles, 8, CONV_WIDTH), _F32),
                   jax.ShapeDtypeStruct(w_out.shape, _BF16)],
        scratch_shapes=[edge_row, edge_row],
        compiler_params=_params(2),
        name="proj_conv",
    )(h, w_conv_bf, w_conv_bf, w_conv_bf, w_conv_bf, conv_w, w_out)


def _attn_kernel(q_ref, k_ref, vt_ref, kc_ref, vtc_ref, ga_ref, o_ref, s_ring, m_ring, p_ring):
    n_chunks, kc = k_ref.shape[2], k_ref.shape[3]
    n_lat, ctx = n_chunks * kc, kc_ref.shape[2]
    tq = s_ring[0].shape[1]
    n_tiles = q_ref.shape[2] // tq
    nt = (((1,), (1,)), ((), ()))

    def fold(v):
        return v.reshape(v.shape[0] // 8, 8, v.shape[1])

    def rows(t):
        return pl.ds(pl.multiple_of(t * tq, tq), tq)

    def key_block(c):
        return (k_ref[0, 0, c], pl.ds(c * kc, kc)) if c < n_chunks else (kc_ref[0, 0], pl.ds(n_lat, ctx))

    def scores(t, g, c):
        keys, span = key_block(c)
        s = lax.dot_general(keys, q_ref[0, g, rows(t), :], nt, preferred_element_type=_F32)
        s_ring[g % 2][span, :] = s
        chunk_max = fold(s).max(axis=0)
        m_buf = m_ring[g % 2]
        m_buf[...] = chunk_max if c == 0 else jnp.maximum(m_buf[...], chunk_max)

    def probs(g, c, m):
        _, span = key_block(c)
        p_ring[g % 2][span, :] = jnp.exp2(s_ring[g % 2][span, :] - m).astype(_BF16)

    def values(t, g):
        p_buf = p_ring[g % 2]
        acc = jnp.dot(vt_ref[0, 0], p_buf[0:n_lat, :], preferred_element_type=_F32)
        acc += jnp.dot(vtc_ref[0, 0], p_buf[n_lat:n_lat + ctx, :], preferred_element_type=_F32)
        o = (acc[0:HEAD_DIM] * (1.0 / acc[HEAD_DIM:HEAD_DIM + 1])).T
        cols = slice(g * HEAD_DIM, (g + 1) * HEAD_DIM)
        o_ref[rows(t), cols] = (o * ga_ref[rows(t), cols].astype(_F32)).astype(_BF16)

    def phase(score_item, prob_head, value_item):
        if prob_head is not None:
            m = m_ring[prob_head % 2][...].max(axis=0, keepdims=True)
        for c in range(n_chunks + 1):
            if score_item is not None:
                scores(*score_item, c)
            if prob_head is not None:
                probs(prob_head, c, m)
        if value_item is not None:
            values(*value_item)

    assert GROUP == 4
    phase((0, 0), None, None)
    phase((0, 1), 0, None)

    def tile(t, carry):
        nxt = jnp.minimum(t + 1, n_tiles - 1)
        phase((t, 2), 1, (t, 0))
        phase((t, 3), 2, (t, 1))
        phase((nxt, 0), 3, (t, 2))
        phase((nxt, 1), 0, (t, 3))
        return carry

    lax.fori_loop(0, n_tiles, tile, 0, unroll=4)


def _attention(q, k, vt, k_ctx, vt_ctx, ga, batch, seq):
    n_chunks = seq // KEY_CHUNK
    ctx = k_ctx.shape[2]
    k5 = k.reshape(batch, N_KV_HEADS, n_chunks, KEY_CHUNK, HEAD_DIM)
    group_w = GROUP * HEAD_DIM
    n_keys = seq + ctx
    return pl.pallas_call(
        _attn_kernel,
        grid=(batch, N_KV_HEADS),
        in_specs=[pl.BlockSpec((1, GROUP, seq, HEAD_DIM), lambda b, kv: (b, kv, 0, 0)),
                  pl.BlockSpec((1, 1, n_chunks, KEY_CHUNK, HEAD_DIM), lambda b, kv: (b, kv, 0, 0, 0)),
                  pl.BlockSpec((1, 1, V_ROWS, seq), lambda b, kv: (b, kv, 0, 0)),
                  pl.BlockSpec((1, 1, ctx, HEAD_DIM), lambda b, kv: (b, kv, 0, 0)),
                  pl.BlockSpec((1, 1, V_ROWS, ctx), lambda b, kv: (b, kv, 0, 0)),
                  pl.BlockSpec((seq, group_w), lambda b, kv: (b, kv))],
        out_specs=pl.BlockSpec((seq, group_w), lambda b, kv: (b, kv)),
        out_shape=jax.ShapeDtypeStruct((batch * seq, ATTN_WIDTH), _BF16),
        scratch_shapes=[[pltpu.VMEM((n_keys, Q_TILE), _F32)] * 2,
                        [pltpu.VMEM((8, Q_TILE), _F32)] * 2,
                        [pltpu.VMEM((n_keys, Q_TILE), _BF16)] * 2],
        compiler_params=_params(2),
        name="attention",
    )(q, k5, vt, k_ctx, vt_ctx, ga)


def _out_kernel(fix_every, n_tiles, attn_ref, cv_ref, fix_ref, w_ref, x_ref, gate_ref, gf_ref, o_ref):
    i = pl.program_id(0)
    tm = cv_ref.shape[0]
    slab_row = lax.broadcasted_iota(jnp.int32, (HALO_ROWS, 1), 0)
    ends_conv_tile = jnp.logical_and((i % fix_every) == fix_every - 1, i < n_tiles - 1)
    needs_fix = jnp.logical_and(slab_row == HALO_ROWS - 1, ends_conv_tile)
    tail = cv_ref[tm - HALO_ROWS:tm, :].astype(_F32) + jnp.where(needs_fix, fix_ref[0, 0:1, :], 0.0)
    cv = jnp.concatenate([cv_ref[0:tm - HALO_ROWS, :], tail.astype(_BF16)], axis=0)
    mix = jnp.concatenate([attn_ref[...], cv], axis=1)
    y = jnp.dot(mix, w_ref[...], preferred_element_type=_F32)
    xn = x_ref[...] + gate_ref[0] * y
    o_ref[...] = xn * lax.rsqrt(jnp.mean(xn * xn, axis=-1, keepdims=True) + EPS) * gf_ref[...]


def _out_proj(attn, cv, fix, w_out_bf, x2d, gate, gf, seq, tm):
    m, d = x2d.shape
    tiles_per_seq = seq // tm
    fix_every = (m // tm) // fix.shape[0]
    last_fix = fix.shape[0] - 1
    mix_spec = pl.BlockSpec((tm, ATTN_WIDTH), lambda i: (i, 0))
    return pl.pallas_call(
        functools.partial(_out_kernel, fix_every, m // tm),
        grid=(m // tm,),
        in_specs=[mix_spec, mix_spec,
                  pl.BlockSpec((1, 8, CONV_WIDTH),
                               lambda i: (jnp.minimum(i // fix_every + 1, last_fix), 0, 0)),
                  pl.BlockSpec(w_out_bf.shape, lambda i: (0, 0)),
                  pl.BlockSpec((tm, d), lambda i: (i, 0)),
                  pl.BlockSpec((1, 1, d), lambda i: (i // tiles_per_seq, 0, 0)),
                  pl.BlockSpec((1, d), lambda i: (0, 0))],
        out_specs=pl.BlockSpec((tm, d), lambda i: (i, 0)),
        out_shape=jax.ShapeDtypeStruct((m, d), _F32),
        compiler_params=_params(1),
        name="out_proj",
    )(attn, cv, fix, w_out_bf, x2d, gate, gf)


def _rope_tables(seq):
    quarter = HEAD_DIM // 4
    t = np.arange(seq)
    row = (t // GRID_W).astype(np.float64)
    col = (t % GRID_W).astype(np.float64)
    inv = ROPE_THETA ** (-np.arange(quarter, dtype=np.float64) / quarter)
    ang = np.concatenate([row[:, None] * inv, row[:, None] * inv,
                          col[:, None] * inv, col[:, None] * inv], axis=1)
    lane = np.arange(HEAD_DIM) % (2 * quarter)
    sin = np.sin(ang)
    tables = (np.cos(ang), np.where(lane < quarter, -sin, 0.0), np.where(lane >= quarter, sin, 0.0))
    return tuple(jnp.asarray(tab.astype(np.float32)) for tab in tables)


def kernel(x, c, ctx, c_ctx, w_mod, b_mod, norm_g, w_in, q_norm_g, k_norm_g, conv_w, w_out,
           final_norm_g):
    batch, seq, d = x.shape
    ctx_len = ctx.shape[1]
    assert w_mod.shape[0] == 1, "single-layer kernel"
    assert seq % ROW_TILE == 0 and seq % KEY_CHUNK == 0 and seq % Q_TILE == 0

    cond_rows = jnp.concatenate([c, c_ctx[None, :], jnp.zeros((MOD_ROWS - batch - 1, d), _F32)], axis=0)
    mod = _adaln(cond_rows, w_mod[0], b_mod[0][None, :])
    shift = mod[:batch + 1, None, 0 * d:1 * d]
    scale = mod[:batch + 1, None, 1 * d:2 * d]
    gate = mod[:batch, None, 2 * d:3 * d]

    g_in = norm_g[0][None, :]
    q_gain = (q_norm_g[0] * (math.log2(math.e) / math.sqrt(HEAD_DIM)))[None, :]
    k_gain = k_norm_g[0][None, :]

    ctx2d = ctx.reshape(batch * ctx_len, d)
    k_ctx, vt_ctx = _ctx_kv(ctx2d, g_in, scale[batch:], shift[batch:], w_in[0], k_gain, batch, ctx_len)

    x2d = x.reshape(batch * seq, d)
    tables = _rope_tables(seq)
    h, q, k, vt, ga, w_conv_bf = _proj_attn(x2d, g_in, scale[:batch], shift[:batch], w_in[0],
                                            q_gain, k_gain, tables, batch, seq, ROW_TILE)
    cv, fix, w_out_bf = _proj_conv(h, w_conv_bf, conv_w[0], w_out[0], seq, CONV_ROW_TILE)

    attn = _attention(q, k, vt, k_ctx, vt_ctx, ga, batch, seq)
    out = _out_proj(attn, cv, fix, w_out_bf, x2d, gate, final_norm_g[None, :], seq, ROW_TILE)
    return out.reshape(batch, seq, d)
```

```python
import functools
import math

import jax
import jax.numpy as jnp
import numpy as np
from jax import lax
from jax.experimental import pallas as pl
from jax.experimental.pallas import tpu as pltpu

D_MODEL = 2048
CTX_LEN = 256
GRID_W = 64
ATTN_WIDTH = 1024
CONV_WIDTH = 1024
HEAD_DIM = 128
N_HEADS = 8
N_KV_HEADS = 2
GROUP = N_HEADS // N_KV_HEADS
KV_WIDTH = N_KV_HEADS * HEAD_DIM
ROPE_THETA = 10000.0
EPS = 1e-6

LANES = 128
ROW_TILE = 512
CONV_ROW_TILE = 1024
SUB_ROWS = 512
COL_TILE = 512
KEY_CHUNK = 2048
Q_TILE = 256
V_ROWS = HEAD_DIM + 16
HALO_ROWS = 16
MOD_ROWS = 8
MOD_ROW_TILE = 256
VMEM_LIMIT = 56 * 1024 * 1024
assert 2 * KV_WIDTH == COL_TILE and ATTN_WIDTH % COL_TILE == 0 and CONV_WIDTH % COL_TILE == 0

_F32 = jnp.float32
_BF16 = jnp.bfloat16


def _silu(v):
    half = 0.5 * v
    return half + half * jnp.tanh(half)


def _params(n_axes, flags=None):
    return pltpu.CompilerParams(dimension_semantics=("arbitrary",) * n_axes,
                                vmem_limit_bytes=VMEM_LIMIT, flags=flags)


def _adaln_kernel(cond_ref, w_ref, b_ref, o_ref):
    @pl.when(pl.program_id(0) == 0)
    def _():
        o_ref[...] = jnp.broadcast_to(b_ref[...], o_ref.shape)

    s = _silu(cond_ref[0]).astype(_BF16)
    o_ref[...] += jnp.dot(s, w_ref[...].astype(_BF16), preferred_element_type=_F32)


def _adaln(cond_rows, w_mod, b_mod):
    d, n = w_mod.shape
    bands = d // MOD_ROW_TILE
    cond_bands = cond_rows.reshape(MOD_ROWS, bands, MOD_ROW_TILE).transpose(1, 0, 2)
    return pl.pallas_call(
        _adaln_kernel,
        grid=(bands,),
        in_specs=[pl.BlockSpec((1, MOD_ROWS, MOD_ROW_TILE), lambda k: (k, 0, 0)),
                  pl.BlockSpec((MOD_ROW_TILE, n), lambda k: (k, 0)),
                  pl.BlockSpec((1, n), lambda k: (0, 0))],
        out_specs=pl.BlockSpec((MOD_ROWS, n), lambda k: (0, 0)),
        out_shape=jax.ShapeDtypeStruct((MOD_ROWS, n), _F32),
        compiler_params=_params(1),
        name="adaln",
    )(cond_bands, w_mod, b_mod)


def _modulated_norm(x_ref, gin_ref, scale_ref, shift_ref, rows=slice(None)):
    xf = x_ref[rows, :]
    y = xf * lax.rsqrt(jnp.mean(xf * xf, axis=-1, keepdims=True) + EPS) * gin_ref[...]
    return (y * (1.0 + scale_ref[0]) + shift_ref[0]).astype(_BF16)


def _norm_rope(xh, g, rope):
    y = xh * lax.rsqrt(jnp.mean(xh * xh, axis=-1, keepdims=True) + EPS) * g
    if rope is None:
        return y
    cos, sin_lo, sin_hi = rope
    return (y * cos + pltpu.roll(y, 3 * HEAD_DIM // 4, axis=1) * sin_lo
            + pltpu.roll(y, HEAD_DIM // 4, axis=1) * sin_hi)


def _store_kv(acc, g, rope, k_ref, vt_ref, rows=slice(None)):
    for hh in range(N_KV_HEADS):
        xh = acc[:, hh * HEAD_DIM:(hh + 1) * HEAD_DIM]
        k_ref[0, hh, rows, :] = _norm_rope(xh, g, rope).astype(_BF16)
        vh = acc[:, KV_WIDTH + hh * HEAD_DIM:KV_WIDTH + (hh + 1) * HEAD_DIM]
        vt_ref[0, hh, 0:HEAD_DIM, rows] = vh.T.astype(_BF16)
        vt_ref[0, hh, HEAD_DIM:V_ROWS, rows] = jnp.ones((V_ROWS - HEAD_DIM, vh.shape[0]), _BF16)


def _ctx_kv_kernel(x_ref, gin_ref, scale_ref, shift_ref, w_ref, g_ref, k_ref, vt_ref):
    h = _modulated_norm(x_ref, gin_ref, scale_ref, shift_ref)
    acc = jnp.dot(h, w_ref[...].astype(_BF16), preferred_element_type=_F32)
    _store_kv(acc, g_ref[...], None, k_ref, vt_ref)


def _proj_attn_kernel(x_ref, gin_ref, scale_ref, shift_ref, w_ref, qg_ref, kg_ref,
                      cos_ref, slo_ref, shi_ref, wside_ref, h_ref, q_ref, k_ref, vt_ref, ga_ref,
                      wside_bf_ref):
    wside_bf_ref[...] = wside_ref[...].astype(_BF16)
    half = x_ref.shape[0] // 2
    halves = [slice(r0, r0 + half) for r0 in range(0, x_ref.shape[0], half)]
    hs = []
    for rows in halves:
        h = _modulated_norm(x_ref, gin_ref, scale_ref, shift_ref, rows)
        h_ref[rows, :] = h
        hs.append(h)
    ropes = [(cos_ref[rows, :], slo_ref[rows, :], shi_ref[rows, :]) for rows in halves]
    def project(col0, width):
        w = w_ref[:, col0:col0 + width].astype(_BF16)
        return [jnp.dot(h, w, preferred_element_type=_F32) for h in hs]

    for rows, rope, acc in zip(halves, ropes, project(0, ATTN_WIDTH)):
        for hh in range(N_HEADS):
            xh = acc[:, hh * HEAD_DIM:(hh + 1) * HEAD_DIM]
            q_ref[0, hh, rows, :] = _norm_rope(xh, qg_ref[...], rope).astype(_BF16)
    for rows, rope, acc in zip(halves, ropes, project(ATTN_WIDTH, 2 * KV_WIDTH)):
        _store_kv(acc, kg_ref[...], rope, k_ref, vt_ref, rows)
    for rows, acc in zip(halves, project(ATTN_WIDTH + 2 * KV_WIDTH, ATTN_WIDTH)):
        ga_ref[rows, :] = _silu(acc).astype(_BF16)


def _proj_conv_kernel(tiles_per_seq, h_ref, wb_ref, wcg_ref, wh_ref, wgc_ref, cw_ref, wside_ref,
                      cv_ref, fix_ref, wside_bf_ref, u_last, e_last):
    i = pl.program_id(1)
    tm = h_ref.shape[0]
    wside_bf_ref[...] = wside_ref[...].astype(_BF16)

    @pl.when(i == 0)
    def _():
        u_last[...] = jnp.zeros(u_last.shape, _F32)
        e_last[...] = jnp.zeros(e_last.shape, _F32)

    def project(w_ref):
        return jnp.concatenate(
            [jnp.dot(h_ref[r0:r0 + SUB_ROWS, :], w_ref[...], preferred_element_type=_F32)
             for r0 in range(0, tm, SUB_ROWS)], axis=0)

    u = project(wcg_ref) * project(wh_ref)
    e = project(wb_ref) * _silu(project(wgc_ref))
    w0, w1, w2 = cw_ref[0:1, :], cw_ref[1:2, :], cw_ref[2:3, :]
    seq_start = (i % tiles_per_seq) == 0
    prev_row = jnp.where(seq_start, 0.0, u_last[...])
    slab_row = lax.broadcasted_iota(jnp.int32, (8, 1), 0)
    down, up = pltpu.roll(u, 1, axis=0), pltpu.roll(u, tm - 1, axis=0)
    u_before = jnp.concatenate([jnp.where(slab_row == 0, prev_row, down[0:8]), down[8:]], axis=0)
    u_after = jnp.concatenate([up[:tm - 8], jnp.where(slab_row == 7, 0.0, up[tm - 8:])], axis=0)
    cv_ref[...] = (e * (u_before * w0 + u * w1 + u_after * w2)).astype(_BF16)
    fix = jnp.where(seq_start, 0.0, e_last[...] * w2 * u[0:1])
    fix_ref[0] = jnp.where(slab_row == 0, fix, 0.0)
    u_last[...] = u[tm - 1:tm]
    e_last[...] = e[tm - 1:tm]


def _rope_specs(tm, seq):
    tiles = seq // tm
    return [pl.BlockSpec((tm, HEAD_DIM), lambda i, *_: (i % tiles, 0))] * 3


def _proj_attn(x2d, gin, scale, shift, w_in, q_gain, k_gain, tables, batch, seq, tm):
    m, d = x2d.shape
    tiles = seq // tm
    steps = m // tm
    n_cols = 2 * ATTN_WIDTH + 2 * KV_WIDTH
    side_cols = (w_in.shape[1] - n_cols) // steps
    assert side_cols % LANES == 0 and n_cols % side_cols == 0
    side_first = n_cols // side_cols
    mod_spec = pl.BlockSpec((1, 1, d), lambda i: (i // tiles, 0, 0))
    gain_spec = pl.BlockSpec((1, HEAD_DIM), lambda i: (0, 0))
    return pl.pallas_call(
        _proj_attn_kernel,
        grid=(steps,),
        in_specs=[pl.BlockSpec((tm, d), lambda i: (i, 0)),
                  pl.BlockSpec((1, d), lambda i: (0, 0)),
                  mod_spec, mod_spec,
                  pl.BlockSpec((d, n_cols), lambda i: (0, 0)),
                  gain_spec, gain_spec] + _rope_specs(tm, seq)
                 + [pl.BlockSpec((d, side_cols), lambda i: (0, side_first + i))],
        out_specs=[pl.BlockSpec((tm, d), lambda i: (i, 0)),
                   pl.BlockSpec((1, N_HEADS, tm, HEAD_DIM), lambda i: (i // tiles, 0, i % tiles, 0)),
                   pl.BlockSpec((1, N_KV_HEADS, tm, HEAD_DIM), lambda i: (i // tiles, 0, i % tiles, 0)),
                   pl.BlockSpec((1, N_KV_HEADS, V_ROWS, tm), lambda i: (i // tiles, 0, 0, i % tiles)),
                   pl.BlockSpec((tm, ATTN_WIDTH), lambda i: (i, 0)),
                   pl.BlockSpec((d, side_cols), lambda i: (0, i))],
        out_shape=[jax.ShapeDtypeStruct((m, d), _BF16),
                   jax.ShapeDtypeStruct((batch, N_HEADS, seq, HEAD_DIM), _BF16),
                   jax.ShapeDtypeStruct((batch, N_KV_HEADS, seq, HEAD_DIM), _BF16),
                   jax.ShapeDtypeStruct((batch, N_KV_HEADS, V_ROWS, seq), _BF16),
                   jax.ShapeDtypeStruct((m, ATTN_WIDTH), _BF16),
                   jax.ShapeDtypeStruct((d, side_cols * steps), _BF16)],
        compiler_params=_params(1),
        name="proj_attn",
    )(x2d, gin, scale, shift, w_in, q_gain, k_gain, *tables, w_in)


def _ctx_kv(ctx2d, gin, scale, shift, w_in, k_gain, batch, ctx_len):
    d = ctx2d.shape[1]
    kv_col_block = ATTN_WIDTH // COL_TILE
    mod_spec = pl.BlockSpec((1, 1, d), lambda i: (0, 0, 0))
    return pl.pallas_call(
        _ctx_kv_kernel,
        grid=(batch,),
        in_specs=[pl.BlockSpec((ctx_len, d), lambda i: (i, 0)),
                  pl.BlockSpec((1, d), lambda i: (0, 0)),
                  mod_spec, mod_spec,
                  pl.BlockSpec((d, 2 * KV_WIDTH), lambda i: (0, kv_col_block)),
                  pl.BlockSpec((1, HEAD_DIM), lambda i: (0, 0))],
        out_specs=[pl.BlockSpec((1, N_KV_HEADS, ctx_len, HEAD_DIM), lambda i: (i, 0, 0, 0)),
                   pl.BlockSpec((1, N_KV_HEADS, V_ROWS, ctx_len), lambda i: (i, 0, 0, 0))],
        out_shape=[jax.ShapeDtypeStruct((batch, N_KV_HEADS, ctx_len, HEAD_DIM), _BF16),
                   jax.ShapeDtypeStruct((batch, N_KV_HEADS, V_ROWS, ctx_len), _BF16)],
        compiler_params=_params(1),
        name="ctx_kv",
    )(ctx2d, gin, scale, shift, w_in, k_gain)


def _proj_conv(h, w_conv_bf, conv_w, w_out, seq, tm):
    m, d = h.shape
    step = CONV_WIDTH // COL_TILE
    row_tiles = m // tm
    side_rows = w_out.shape[0] // (step * row_tiles)
    assert side_rows % HALO_ROWS == 0 and seq % tm == 0

    def w_spec(group):
        return pl.BlockSpec((d, COL_TILE), lambda j, i: (0, group * step + j))

    side_spec = pl.BlockSpec((side_rows, w_out.shape[1]), lambda j, i: (j * row_tiles + i, 0))
    edge_row = pltpu.VMEM((1, COL_TILE), _F32)
    return pl.pallas_call(
        functools.partial(_proj_conv_kernel, seq // tm),
        grid=(step, row_tiles),
        in_specs=[pl.BlockSpec((tm, d), lambda j, i: (i, 0)),
                  w_spec(0), w_spec(1), w_spec(2), w_spec(3),
                  pl.BlockSpec((3, COL_TILE), lambda j, i: (0, j)), side_spec],
        out_specs=[pl.BlockSpec((tm, COL_TILE), lambda j, i: (i, j)),
                   pl.BlockSpec((1, 8, COL_TILE), lambda j, i: (i, 0, j)),
                   side_spec],
        out_shape=[jax.ShapeDtypeStruct((m, CONV_WIDTH), _BF16),
                   jax.ShapeDtypeStruct((row_tiles, 8, CONV_WIDTH), _F32),
                   jax.ShapeDtypeStruct(w_out.shape, _BF16)],
        scratch_shapes=[edge_row, edge_row],
        compiler_params=_params(2),
        name="proj_conv",
    )(h, w_conv_bf, w_conv_bf, w_conv_bf, w_conv_bf, conv_w, w_out)


def _attn_kernel(q_ref, k_ref, vt_ref, kc_ref, vtc_ref, ga_ref, o_ref, s_ring, m_ring, p_ring):
    n_chunks, kc = k_ref.shape[2], k_ref.shape[3]
    n_lat, ctx = n_chunks * kc, kc_ref.shape[2]
    tq = s_ring[0].shape[1]
    n_tiles = q_ref.shape[2] // tq
    nt = (((1,), (1,)), ((), ()))

    def fold(v):
        return v.reshape(v.shape[0] // 8, 8, v.shape[1])

    def rows(t):
        return pl.ds(pl.multiple_of(t * tq, tq), tq)

    def key_block(c):
        return (k_ref[0, 0, c], pl.ds(c * kc, kc)) if c < n_chunks else (kc_ref[0, 0], pl.ds(n_lat, ctx))

    def scores(t, g, c):
        keys, span = key_block(c)
        s = lax.dot_general(keys, q_ref[0, g, rows(t), :], nt, preferred_element_type=_F32)
        s_ring[g % 2][span, :] = s
        chunk_max = fold(s).max(axis=0)
        m_buf = m_ring[g % 2]
        m_buf[...] = chunk_max if c == 0 else jnp.maximum(m_buf[...], chunk_max)

    def probs(g, c, m):
        _, span = key_block(c)
        p_ring[g % 2][span, :] = jnp.exp2(s_ring[g % 2][span, :] - m).astype(_BF16)

    def values(t, g):
        p_buf = p_ring[g % 2]
        acc = jnp.dot(vt_ref[0, 0], p_buf[0:n_lat, :], preferred_element_type=_F32)
        acc += jnp.dot(vtc_ref[0, 0], p_buf[n_lat:n_lat + ctx, :], preferred_element_type=_F32)
        o = (acc[0:HEAD_DIM] * (1.0 / acc[HEAD_DIM:HEAD_DIM + 1])).T
        cols = slice(g * HEAD_DIM, (g + 1) * HEAD_DIM)
        o_ref[rows(t), cols] = (o * ga_ref[rows(t), cols].astype(_F32)).astype(_BF16)

    def phase(score_item, prob_head, value_item):
        if prob_head is not None:
            m = m_ring[prob_head % 2][...].max(axis=0, keepdims=True)
        for c in range(n_chunks + 1):
            if score_item is not None:
                scores(*score_item, c)
            if prob_head is not None:
                probs(prob_head, c, m)
        if value_item is not None:
            values(*value_item)

    assert GROUP == 4
    phase((0, 0), None, None)
    phase((0, 1), 0, None)

    def tile(t, carry):
        nxt = jnp.minimum(t + 1, n_tiles - 1)
        phase((t, 2), 1, (t, 0))
        phase((t, 3), 2, (t, 1))
        phase((nxt, 0), 3, (t, 2))
        phase((nxt, 1), 0, (t, 3))
        return carry

    lax.fori_loop(0, n_tiles, tile, 0, unroll=4)


def _attention(q, k, vt, k_ctx, vt_ctx, ga, batch, seq):
    n_chunks = seq // KEY_CHUNK
    ctx = k_ctx.shape[2]
    k5 = k.reshape(batch, N_KV_HEADS, n_chunks, KEY_CHUNK, HEAD_DIM)
    group_w = GROUP * HEAD_DIM
    n_keys = seq + ctx
    return pl.pallas_call(
        _attn_kernel,
        grid=(batch, N_KV_HEADS),
        in_specs=[pl.BlockSpec((1, GROUP, seq, HEAD_DIM), lambda b, kv: (b, kv, 0, 0)),
                  pl.BlockSpec((1, 1, n_chunks, KEY_CHUNK, HEAD_DIM), lambda b, kv: (b, kv, 0, 0, 0)),
                  pl.BlockSpec((1, 1, V_ROWS, seq), lambda b, kv: (b, kv, 0, 0)),
                  pl.BlockSpec((1, 1, ctx, HEAD_DIM), lambda b, kv: (b, kv, 0, 0)),
                  pl.BlockSpec((1, 1, V_ROWS, ctx), lambda b, kv: (b, kv, 0, 0)),
                  pl.BlockSpec((seq, group_w), lambda b, kv: (b, kv))],
        out_specs=pl.BlockSpec((seq, group_w), lambda b, kv: (b, kv)),
        out_shape=jax.ShapeDtypeStruct((batch * seq, ATTN_WIDTH), _BF16),
        scratch_shapes=[[pltpu.VMEM((n_keys, Q_TILE), _F32)] * 2,
                        [pltpu.VMEM((8, Q_TILE), _F32)] * 2,
                        [pltpu.VMEM((n_keys, Q_TILE), _BF16)] * 2],
        compiler_params=_params(2),
        name="attention",
    )(q, k5, vt, k_ctx, vt_ctx, ga)


def _out_kernel(fix_every, n_tiles, attn_ref, cv_ref, fix_ref, w_ref, x_ref, gate_ref, gf_ref, o_ref):
    i = pl.program_id(0)
    tm = cv_ref.shape[0]
    slab_row = lax.broadcasted_iota(jnp.int32, (HALO_ROWS, 1), 0)
    ends_conv_tile = jnp.logical_and((i % fix_every) == fix_every - 1, i < n_tiles - 1)
    needs_fix = jnp.logical_and(slab_row == HALO_ROWS - 1, ends_conv_tile)
    tail = cv_ref[tm - HALO_ROWS:tm, :].astype(_F32) + jnp.where(needs_fix, fix_ref[0, 0:1, :], 0.0)
    cv = jnp.concatenate([cv_ref[0:tm - HALO_ROWS, :], tail.astype(_BF16)], axis=0)
    mix = jnp.concatenate([attn_ref[...], cv], axis=1)
    y = jnp.dot(mix, w_ref[...], preferred_element_type=_F32)
    xn = x_ref[...] + gate_ref[0] * y
    o_ref[...] = xn * lax.rsqrt(jnp.mean(xn * xn, axis=-1, keepdims=True) + EPS) * gf_ref[...]


def _out_proj(attn, cv, fix, w_out_bf, x2d, gate, gf, seq, tm):
    m, d = x2d.shape
    tiles_per_seq = seq // tm
    fix_every = (m // tm) // fix.shape[0]
    last_fix = fix.shape[0] - 1
    mix_spec = pl.BlockSpec((tm, ATTN_WIDTH), lambda i: (i, 0))
    return pl.pallas_call(
        functools.partial(_out_kernel, fix_every, m // tm),
        grid=(m // tm,),
        in_specs=[mix_spec, mix_spec,
                  pl.BlockSpec((1, 8, CONV_WIDTH),
                               lambda i: (jnp.minimum(i // fix_every + 1, last_fix), 0, 0)),
                  pl.BlockSpec(w_out_bf.shape, lambda i: (0, 0)),
                  pl.BlockSpec((tm, d), lambda i: (i, 0)),
                  pl.BlockSpec((1, 1, d), lambda i: (i // tiles_per_seq, 0, 0)),
                  pl.BlockSpec((1, d), lambda i: (0, 0))],
        out_specs=pl.BlockSpec((tm, d), lambda i: (i, 0)),
        out_shape=jax.ShapeDtypeStruct((m, d), _F32),
        compiler_params=_params(1),
        name="out_proj",
    )(attn, cv, fix, w_out_bf, x2d, gate, gf)


def _rope_tables(seq):
    quarter = HEAD_DIM // 4
    t = np.arange(seq)
    row = (t // GRID_W).astype(np.float64)
    col = (t % GRID_W).astype(np.float64)
    inv = ROPE_THETA ** (-np.arange(quarter, dtype=np.float64) / quarter)
    ang = np.concatenate([row[:, None] * inv, row[:, None] * inv,
                          col[:, None] * inv, col[:, None] * inv], axis=1)
    lane = np.arange(HEAD_DIM) % (2 * quarter)
    sin = np.sin(ang)
    tables = (np.cos(ang), np.where(lane < quarter, -sin, 0.0), np.where(lane >= quarter, sin, 0.0))
    return tuple(jnp.asarray(tab.astype(np.float32)) for tab in tables)


def kernel(x, c, ctx, c_ctx, w_mod, b_mod, norm_g, w_in, q_norm_g, k_norm_g, conv_w, w_out,
           final_norm_g):
    batch, seq, d = x.shape
    ctx_len = ctx.shape[1]
    assert w_mod.shape[0] == 1, "single-layer kernel"
    assert seq % ROW_TILE == 0 and seq % KEY_CHUNK == 0 and seq % Q_TILE == 0

    cond_rows = jnp.concatenate([c, c_ctx[None, :], jnp.zeros((MOD_ROWS - batch - 1, d), _F32)], axis=0)
    mod = _adaln(cond_rows, w_mod[0], b_mod[0][None, :])
    shift = mod[:batch + 1, None, 0 * d:1 * d]
    scale = mod[:batch + 1, None, 1 * d:2 * d]
    gate = mod[:batch, None, 2 * d:3 * d]

    g_in = norm_g[0][None, :]
    q_gain = (q_norm_g[0] * (math.log2(math.e) / math.sqrt(HEAD_DIM)))[None, :]
    k_gain = k_norm_g[0][None, :]

    ctx2d = ctx.reshape(batch * ctx_len, d)
    k_ctx, vt_ctx = _ctx_kv(ctx2d, g_in, scale[batch:], shift[batch:], w_in[0], k_gain, batch, ctx_len)

    x2d = x.reshape(batch * seq, d)
    tables = _rope_tables(seq)
    h, q, k, vt, ga, w_conv_bf = _proj_attn(x2d, g_in, scale[:batch], shift[:batch], w_in[0],
                                            q_gain, k_gain, tables, batch, seq, ROW_TILE)
    cv, fix, w_out_bf = _proj_conv(h, w_conv_bf, conv_w[0], w_out[0], seq, CONV_ROW_TILE)

    attn = _attention(q, k, vt, k_ctx, vt_ctx, ga, batch, seq)
    out = _out_proj(attn, cv, fix, w_out_bf, x2d, gate, final_norm_g[None, :], seq, ROW_TILE)
    return out.reshape(batch, seq, d)
```

```python
import functools
import math

import jax
import jax.numpy as jnp
import numpy as np
from jax import lax
from jax.experimental import pallas as pl
from jax.experimental.pallas import tpu as pltpu

D_MODEL = 2048
CTX_LEN = 256
GRID_W = 64
ATTN_WIDTH = 1024
CONV_WIDTH = 1024
HEAD_DIM = 128
N_HEADS = 8
N_KV_HEADS = 2
GROUP = N_HEADS // N_KV_HEADS
KV_WIDTH = N_KV_HEADS * HEAD_DIM
ROPE_THETA = 10000.0
EPS = 1e-6

LANES = 128
ROW_TILE = 512
CONV_ROW_TILE = 1024
SUB_ROWS = 512
COL_TILE = 512
KEY_CHUNK = 2048
Q_TILE = 256
V_ROWS = HEAD_DIM + 16
HALO_ROWS = 16
MOD_ROWS = 8
MOD_ROW_TILE = 256
VMEM_LIMIT = 56 * 1024 * 1024
assert 2 * KV_WIDTH == COL_TILE and ATTN_WIDTH % COL_TILE == 0 and CONV_WIDTH % COL_TILE == 0

_F32 = jnp.float32
_BF16 = jnp.bfloat16


def _silu(v):
    half = 0.5 * v
    return half + half * jnp.tanh(half)


def _params(n_axes, flags=None):
    return pltpu.CompilerParams(dimension_semantics=("arbitrary",) * n_axes,
                                vmem_limit_bytes=VMEM_LIMIT, flags=flags)


def _adaln_kernel(cond_ref, w_ref, b_ref, o_ref):
    @pl.when(pl.program_id(0) == 0)
    def _():
        o_ref[...] = jnp.broadcast_to(b_ref[...], o_ref.shape)

    s = _silu(cond_ref[0]).astype(_BF16)
    o_ref[...] += jnp.dot(s, w_ref[...].astype(_BF16), preferred_element_type=_F32)


def _adaln(cond_rows, w_mod, b_mod):
    d, n = w_mod.shape
    bands = d // MOD_ROW_TILE
    cond_bands = cond_rows.reshape(MOD_ROWS, bands, MOD_ROW_TILE).transpose(1, 0, 2)
    return pl.pallas_call(
        _adaln_kernel,
        grid=(bands,),
        in_specs=[pl.BlockSpec((1, MOD_ROWS, MOD_ROW_TILE), lambda k: (k, 0, 0)),
                  pl.BlockSpec((MOD_ROW_TILE, n), lambda k: (k, 0)),
                  pl.BlockSpec((1, n), lambda k: (0, 0))],
        out_specs=pl.BlockSpec((MOD_ROWS, n), lambda k: (0, 0)),
        out_shape=jax.ShapeDtypeStruct((MOD_ROWS, n), _F32),
        compiler_params=_params(1),
        name="adaln",
    )(cond_bands, w_mod, b_mod)


def _modulated_norm(x_ref, gin_ref, scale_ref, shift_ref, rows=slice(None)):
    xf = x_ref[rows, :]
    y = xf * lax.rsqrt(jnp.mean(xf * xf, axis=-1, keepdims=True) + EPS) * gin_ref[...]
    return (y * (1.0 + scale_ref[0]) + shift_ref[0]).astype(_BF16)


def _norm_rope(xh, g, rope):
    y = xh * lax.rsqrt(jnp.mean(xh * xh, axis=-1, keepdims=True) + EPS) * g
    if rope is None:
        return y
    cos, sin_lo, sin_hi = rope
    return (y * cos + pltpu.roll(y, 3 * HEAD_DIM // 4, axis=1) * sin_lo
            + pltpu.roll(y, HEAD_DIM // 4, axis=1) * sin_hi)


def _store_kv(acc, g, rope, k_ref, vt_ref, rows=slice(None)):
    for hh in range(N_KV_HEADS):
        xh = acc[:, hh * HEAD_DIM:(hh + 1) * HEAD_DIM]
        k_ref[0, hh, rows, :] = _norm_rope(xh, g, rope).astype(_BF16)
        vh = acc[:, KV_WIDTH + hh * HEAD_DIM:KV_WIDTH + (hh + 1) * HEAD_DIM]
        vt_ref[0, hh, 0:HEAD_DIM, rows] = vh.T.astype(_BF16)
        vt_ref[0, hh, HEAD_DIM:V_ROWS, rows] = jnp.ones((V_ROWS - HEAD_DIM, vh.shape[0]), _BF16)


def _ctx_kv_kernel(x_ref, gin_ref, scale_ref, shift_ref, w_ref, g_ref, k_ref, vt_ref):
    h = _modulated_norm(x_ref, gin_ref, scale_ref, shift_ref)
    acc = jnp.dot(h, w_ref[...].astype(_BF16), preferred_element_type=_F32)
    ctx_len = k_ref.shape[2]
    for b in range(k_ref.shape[0]):
        _store_kv(acc[b * ctx_len:(b + 1) * ctx_len], g_ref[...], None, k_ref.at[b:b + 1], vt_ref.at[b:b + 1])


def _proj_attn_kernel(x_ref, gin_ref, scale_ref, shift_ref, w_ref, qg_ref, kg_ref,
                      cos_ref, slo_ref, shi_ref, wside_ref, h_ref, q_ref, k_ref, vt_ref, ga_ref,
                      wside_bf_ref):
    wside_bf_ref[...] = wside_ref[...].astype(_BF16)
    half = x_ref.shape[0] // 2
    halves = [slice(r0, r0 + half) for r0 in range(0, x_ref.shape[0], half)]
    hs = []
    for rows in halves:
        h = _modulated_norm(x_ref, gin_ref, scale_ref, shift_ref, rows)
        h_ref[rows, :] = h
        hs.append(h)
    ropes = [(cos_ref[rows, :], slo_ref[rows, :], shi_ref[rows, :]) for rows in halves]
    def project(col0, width):
        w = w_ref[:, col0:col0 + width].astype(_BF16)
        return [jnp.dot(h, w, preferred_element_type=_F32) for h in hs]

    for rows, rope, acc in zip(halves, ropes, project(0, ATTN_WIDTH)):
        for hh in range(N_HEADS):
            xh = acc[:, hh * HEAD_DIM:(hh + 1) * HEAD_DIM]
            q_ref[0, hh, rows, :] = _norm_rope(xh, qg_ref[...], rope).astype(_BF16)
    for rows, rope, acc in zip(halves, ropes, project(ATTN_WIDTH, 2 * KV_WIDTH)):
        _store_kv(acc, kg_ref[...], rope, k_ref, vt_ref, rows)
    for rows, acc in zip(halves, project(ATTN_WIDTH + 2 * KV_WIDTH, ATTN_WIDTH)):
        ga_ref[rows, :] = _silu(acc).astype(_BF16)


def _proj_conv_kernel(tiles_per_seq, h_ref, wb_ref, wcg_ref, wh_ref, wgc_ref, cw_ref, wside_ref,
                      cv_ref, fix_ref, wside_bf_ref, u_last, e_last):
    i = pl.program_id(1)
    tm = h_ref.shape[0]
    wside_bf_ref[...] = wside_ref[...].astype(_BF16)

    @pl.when(i == 0)
    def _():
        u_last[...] = jnp.zeros(u_last.shape, _F32)
        e_last[...] = jnp.zeros(e_last.shape, _F32)

    def project(w_ref):
        return jnp.concatenate(
            [jnp.dot(h_ref[r0:r0 + SUB_ROWS, :], w_ref[...], preferred_element_type=_F32)
             for r0 in range(0, tm, SUB_ROWS)], axis=0)

    u = project(wcg_ref) * project(wh_ref)
    e = project(wb_ref) * _silu(project(wgc_ref))
    w0, w1, w2 = cw_ref[0:1, :], cw_ref[1:2, :], cw_ref[2:3, :]
    seq_start = (i % tiles_per_seq) == 0
    prev_row = jnp.where(seq_start, 0.0, u_last[...])
    slab_row = lax.broadcasted_iota(jnp.int32, (8, 1), 0)
    down, up = pltpu.roll(u, 1, axis=0), pltpu.roll(u, tm - 1, axis=0)
    u_before = jnp.concatenate([jnp.where(slab_row == 0, prev_row, down[0:8]), down[8:]], axis=0)
    u_after = jnp.concatenate([up[:tm - 8], jnp.where(slab_row == 7, 0.0, up[tm - 8:])], axis=0)
    cv_ref[...] = (e * (u_before * w0 + u * w1 + u_after * w2)).astype(_BF16)
    fix = jnp.where(seq_start, 0.0, e_last[...] * w2 * u[0:1])
    fix_ref[0] = jnp.where(slab_row == 0, fix, 0.0)
    u_last[...] = u[tm - 1:tm]
    e_last[...] = e[tm - 1:tm]


def _rope_specs(tm, seq):
    tiles = seq // tm
    return [pl.BlockSpec((tm, HEAD_DIM), lambda i, *_: (i % tiles, 0))] * 3


def _proj_attn(x2d, gin, scale, shift, w_in, q_gain, k_gain, tables, batch, seq, tm):
    m, d = x2d.shape
    tiles = seq // tm
    steps = m // tm
    n_cols = 2 * ATTN_WIDTH + 2 * KV_WIDTH
    side_cols = (w_in.shape[1] - n_cols) // steps
    assert side_cols % LANES == 0 and n_cols % side_cols == 0
    side_first = n_cols // side_cols
    mod_spec = pl.BlockSpec((1, 1, d), lambda i: (i // tiles, 0, 0))
    gain_spec = pl.BlockSpec((1, HEAD_DIM), lambda i: (0, 0))
    return pl.pallas_call(
        _proj_attn_kernel,
        grid=(steps,),
        in_specs=[pl.BlockSpec((tm, d), lambda i: (i, 0)),
                  pl.BlockSpec((1, d), lambda i: (0, 0)),
                  mod_spec, mod_spec,
                  pl.BlockSpec((d, n_cols), lambda i: (0, 0)),
                  gain_spec, gain_spec] + _rope_specs(tm, seq)
                 + [pl.BlockSpec((d, side_cols), lambda i: (0, side_first + i))],
        out_specs=[pl.BlockSpec((tm, d), lambda i: (i, 0)),
                   pl.BlockSpec((1, N_HEADS, tm, HEAD_DIM), lambda i: (i // tiles, 0, i % tiles, 0)),
                   pl.BlockSpec((1, N_KV_HEADS, tm, HEAD_DIM), lambda i: (i // tiles, 0, i % tiles, 0)),
                   pl.BlockSpec((1, N_KV_HEADS, V_ROWS, tm), lambda i: (i // tiles, 0, 0, i % tiles)),
                   pl.BlockSpec((tm, ATTN_WIDTH), lambda i: (i, 0)),
                   pl.BlockSpec((d, side_cols), lambda i: (0, i))],
        out_shape=[jax.ShapeDtypeStruct((m, d), _BF16),
                   jax.ShapeDtypeStruct((batch, N_HEADS, seq, HEAD_DIM), _BF16),
                   jax.ShapeDtypeStruct((batch, N_KV_HEADS, seq, HEAD_DIM), _BF16),
                   jax.ShapeDtypeStruct((batch, N_KV_HEADS, V_ROWS, seq), _BF16),
                   jax.ShapeDtypeStruct((m, ATTN_WIDTH), _BF16),
                   jax.ShapeDtypeStruct((d, side_cols * steps), _BF16)],
        compiler_params=_params(1),
        name="proj_attn",
    )(x2d, gin, scale, shift, w_in, q_gain, k_gain, *tables, w_in)


def _ctx_kv(ctx2d, gin, scale, shift, w_in, k_gain, batch, ctx_len):
    d = ctx2d.shape[1]
    kv_col_block = ATTN_WIDTH // COL_TILE
    mod_spec = pl.BlockSpec((1, 1, d), lambda i: (0, 0, 0))
    return pl.pallas_call(
        _ctx_kv_kernel,
        grid=(1,),
        in_specs=[pl.BlockSpec((batch * ctx_len, d), lambda i: (0, 0)),
                  pl.BlockSpec((1, d), lambda i: (0, 0)),
                  mod_spec, mod_spec,
                  pl.BlockSpec((d, 2 * KV_WIDTH), lambda i: (0, kv_col_block)),
                  pl.BlockSpec((1, HEAD_DIM), lambda i: (0, 0))],
        out_specs=[pl.BlockSpec((batch, N_KV_HEADS, ctx_len, HEAD_DIM), lambda i: (0, 0, 0, 0)),
                   pl.BlockSpec((batch, N_KV_HEADS, V_ROWS, ctx_len), lambda i: (0, 0, 0, 0))],
        out_shape=[jax.ShapeDtypeStruct((batch, N_KV_HEADS, ctx_len, HEAD_DIM), _BF16),
                   jax.ShapeDtypeStruct((batch, N_KV_HEADS, V_ROWS, ctx_len), _BF16)],
        compiler_params=_params(1),
        name="ctx_kv",
    )(ctx2d, gin, scale, shift, w_in, k_gain)


def _proj_conv(h, w_conv_bf, conv_w, w_out, seq, tm):
    m, d = h.shape
    step = CONV_WIDTH // COL_TILE
    row_tiles = m // tm
    side_rows = w_out.shape[0] // (step * row_tiles)
    assert side_rows % HALO_ROWS == 0 and seq % tm == 0

    def w_spec(group):
        return pl.BlockSpec((d, COL_TILE), lambda j, i: (0, group * step + j))

    side_spec = pl.BlockSpec((side_rows, w_out.shape[1]), lambda j, i: (j * row_tiles + i, 0))
    edge_row = pltpu.VMEM((1, COL_TILE), _F32)
    return pl.pallas_call(
        functools.partial(_proj_conv_kernel, seq // tm),
        grid=(step, row_tiles),
        in_specs=[pl.BlockSpec((tm, d), lambda j, i: (i, 0)),
                  w_spec(0), w_spec(1), w_spec(2), w_spec(3),
                  pl.BlockSpec((3, COL_TILE), lambda j, i: (0, j)), side_spec],
        out_specs=[pl.BlockSpec((tm, COL_TILE), lambda j, i: (i, j)),
                   pl.BlockSpec((1, 8, COL_TILE), lambda j, i: (i, 0, j)),
                   side_spec],
        out_shape=[jax.ShapeDtypeStruct((m, CONV_WIDTH), _BF16),
                   jax.ShapeDtypeStruct((row_tiles, 8, CONV_WIDTH), _F32),
                   jax.ShapeDtypeStruct(w_out.shape, _BF16)],
        scratch_shapes=[edge_row, edge_row],
        compiler_params=_params(2),
        name="proj_conv",
    )(h, w_conv_bf, w_conv_bf, w_conv_bf, w_conv_bf, conv_w, w_out)


def _attn_kernel(q_ref, k_ref, vt_ref, kc_ref, vtc_ref, ga_ref, o_ref, s_ring, m_ring, p_ring):
    n_chunks, kc = k_ref.shape[2], k_ref.shape[3]
    n_lat, ctx = n_chunks * kc, kc_ref.shape[2]
    tq = s_ring[0].shape[1]
    n_tiles = q_ref.shape[2] // tq
    nt = (((1,), (1,)), ((), ()))

    def fold(v):
        return v.reshape(v.shape[0] // 8, 8, v.shape[1])

    def rows(t):
        return pl.ds(pl.multiple_of(t * tq, tq), tq)

    def key_block(c):
        return (k_ref[0, 0, c], pl.ds(c * kc, kc)) if c < n_chunks else (kc_ref[0, 0], pl.ds(n_lat, ctx))

    def scores(t, g, c):
        keys, span = key_block(c)
        s = lax.dot_general(keys, q_ref[0, g, rows(t), :], nt, preferred_element_type=_F32)
        s_ring[g % 2][span, :] = s
        chunk_max = fold(s).max(axis=0)
        m_buf = m_ring[g % 2]
        m_buf[...] = chunk_max if c == 0 else jnp.maximum(m_buf[...], chunk_max)

    def probs(g, c, m):
        _, span = key_block(c)
        p_ring[g % 2][span, :] = jnp.exp2(s_ring[g % 2][span, :] - m).astype(_BF16)

    def values(t, g):
        p_buf = p_ring[g % 2]
        acc = jnp.dot(vt_ref[0, 0], p_buf[0:n_lat, :], preferred_element_type=_F32)
        acc += jnp.dot(vtc_ref[0, 0], p_buf[n_lat:n_lat + ctx, :], preferred_element_type=_F32)
        o = (acc[0:HEAD_DIM] * (1.0 / acc[HEAD_DIM:HEAD_DIM + 1])).T
        cols = slice(g * HEAD_DIM, (g + 1) * HEAD_DIM)
        o_ref[rows(t), cols] = (o * ga_ref[rows(t), cols].astype(_F32)).astype(_BF16)

    def phase(score_item, prob_head, value_item):
        if prob_head is not None:
            m = m_ring[prob_head % 2][...].max(axis=0, keepdims=True)
        for c in range(n_chunks + 1):
            if score_item is not None:
                scores(*score_item, c)
            if prob_head is not None:
                probs(prob_head, c, m)
        if value_item is not None:
            values(*value_item)

    assert GROUP == 4
    phase((0, 0), None, None)
    phase((0, 1), 0, None)

    def tile(t, carry):
        nxt = jnp.minimum(t + 1, n_tiles - 1)
        phase((t, 2), 1, (t, 0))
        phase((t, 3), 2, (t, 1))
        phase((nxt, 0), 3, (t, 2))
        phase((nxt, 1), 0, (t, 3))
        return carry

    lax.fori_loop(0, n_tiles, tile, 0, unroll=4)


def _attention(q, k, vt, k_ctx, vt_ctx, ga, batch, seq):
    n_chunks = seq // KEY_CHUNK
    ctx = k_ctx.shape[2]
    k5 = k.reshape(batch, N_KV_HEADS, n_chunks, KEY_CHUNK, HEAD_DIM)
    group_w = GROUP * HEAD_DIM
    n_keys = seq + ctx
    return pl.pallas_call(
        _attn_kernel,
        grid=(batch, N_KV_HEADS),
        in_specs=[pl.BlockSpec((1, GROUP, seq, HEAD_DIM), lambda b, kv: (b, kv, 0, 0)),
                  pl.BlockSpec((1, 1, n_chunks, KEY_CHUNK, HEAD_DIM), lambda b, kv: (b, kv, 0, 0, 0)),
                  pl.BlockSpec((1, 1, V_ROWS, seq), lambda b, kv: (b, kv, 0, 0)),
                  pl.BlockSpec((1, 1, ctx, HEAD_DIM), lambda b, kv: (b, kv, 0, 0)),
                  pl.BlockSpec((1, 1, V_ROWS, ctx), lambda b, kv: (b, kv, 0, 0)),
                  pl.BlockSpec((seq, group_w), lambda b, kv: (b, kv))],
        out_specs=pl.BlockSpec((seq, group_w), lambda b, kv: (b, kv)),
        out_shape=jax.ShapeDtypeStruct((batch * seq, ATTN_WIDTH), _BF16),
        scratch_shapes=[[pltpu.VMEM((n_keys, Q_TILE), _F32)] * 2,
                        [pltpu.VMEM((8, Q_TILE), _F32)] * 2,
                        [pltpu.VMEM((n_keys, Q_TILE), _BF16)] * 2],
        compiler_params=_params(2),
        name="attention",
    )(q, k5, vt, k_ctx, vt_ctx, ga)


def _out_kernel(fix_every, n_tiles, attn_ref, cv_ref, fix_ref, w_ref, x_ref, gate_ref, gf_ref, o_ref):
    i = pl.program_id(0)
    tm = cv_ref.shape[0]
    slab_row = lax.broadcasted_iota(jnp.int32, (HALO_ROWS, 1), 0)
    ends_conv_tile = jnp.logical_and((i % fix_every) == fix_every - 1, i < n_tiles - 1)
    needs_fix = jnp.logical_and(slab_row == HALO_ROWS - 1, ends_conv_tile)
    tail = cv_ref[tm - HALO_ROWS:tm, :].astype(_F32) + jnp.where(needs_fix, fix_ref[0, 0:1, :], 0.0)
    cv = jnp.concatenate([cv_ref[0:tm - HALO_ROWS, :], tail.astype(_BF16)], axis=0)
    mix = jnp.concatenate([attn_ref[...], cv], axis=1)
    y = jnp.dot(mix, w_ref[...], preferred_element_type=_F32)
    xn = x_ref[...] + gate_ref[0] * y
    o_ref[...] = xn * lax.rsqrt(jnp.mean(xn * xn, axis=-1, keepdims=True) + EPS) * gf_ref[...]


def _out_proj(attn, cv, fix, w_out_bf, x2d, gate, gf, seq, tm):
    m, d = x2d.shape
    tiles_per_seq = seq // tm
    fix_every = (m // tm) // fix.shape[0]
    last_fix = fix.shape[0] - 1
    mix_spec = pl.BlockSpec((tm, ATTN_WIDTH), lambda i: (i, 0))
    return pl.pallas_call(
        functools.partial(_out_kernel, fix_every, m // tm),
        grid=(m // tm,),
        in_specs=[mix_spec, mix_spec,
                  pl.BlockSpec((1, 8, CONV_WIDTH),
                               lambda i: (jnp.minimum(i // fix_every + 1, last_fix), 0, 0)),
                  pl.BlockSpec(w_out_bf.shape, lambda i: (0, 0)),
                  pl.BlockSpec((tm, d), lambda i: (i, 0)),
                  pl.BlockSpec((1, 1, d), lambda i: (i // tiles_per_seq, 0, 0)),
                  pl.BlockSpec((1, d), lambda i: (0, 0))],
        out_specs=pl.BlockSpec((tm, d), lambda i: (i, 0)),
        out_shape=jax.ShapeDtypeStruct((m, d), _F32),
        compiler_params=_params(1),
        name="out_proj",
    )(attn, cv, fix, w_out_bf, x2d, gate, gf)


def _rope_tables(seq):
    quarter = HEAD_DIM // 4
    t = np.arange(seq)
    row = (t // GRID_W).astype(np.float64)
    col = (t % GRID_W).astype(np.float64)
    inv = ROPE_THETA ** (-np.arange(quarter, dtype=np.float64) / quarter)
    ang = np.concatenate([row[:, None] * inv, row[:, None] * inv,
                          col[:, None] * inv, col[:, None] * inv], axis=1)
    lane = np.arange(HEAD_DIM) % (2 * quarter)
    sin = np.sin(ang)
    tables = (np.cos(ang), np.where(lane < quarter, -sin, 0.0), np.where(lane >= quarter, sin, 0.0))
    return tuple(jnp.asarray(tab.astype(np.float32)) for tab in tables)


def kernel(x, c, ctx, c_ctx, w_mod, b_mod, norm_g, w_in, q_norm_g, k_norm_g, conv_w, w_out,
           final_norm_g):
    batch, seq, d = x.shape
    ctx_len = ctx.shape[1]
    assert w_mod.shape[0] == 1, "single-layer kernel"
    assert seq % ROW_TILE == 0 and seq % KEY_CHUNK == 0 and seq % Q_TILE == 0

    cond_rows = jnp.concatenate([c, c_ctx[None, :], jnp.zeros((MOD_ROWS - batch - 1, d), _F32)], axis=0)
    mod = _adaln(cond_rows, w_mod[0], b_mod[0][None, :])
    shift = mod[:batch + 1, None, 0 * d:1 * d]
    scale = mod[:batch + 1, None, 1 * d:2 * d]
    gate = mod[:batch, None, 2 * d:3 * d]

    g_in = norm_g[0][None, :]
    q_gain = (q_norm_g[0] * (math.log2(math.e) / math.sqrt(HEAD_DIM)))[None, :]
    k_gain = k_norm_g[0][None, :]

    ctx2d = ctx.reshape(batch * ctx_len, d)
    k_ctx, vt_ctx = _ctx_kv(ctx2d, g_in, scale[batch:], shift[batch:], w_in[0], k_gain, batch, ctx_len)

    x2d = x.reshape(batch * seq, d)
    tables = _rope_tables(seq)
    h, q, k, vt, ga, w_conv_bf = _proj_attn(x2d, g_in, scale[:batch], shift[:batch], w_in[0],
                                            q_gain, k_gain, tables, batch, seq, ROW_TILE)
    cv, fix, w_out_bf = _proj_conv(h, w_conv_bf, conv_w[0], w_out[0], seq, CONV_ROW_TILE)

    attn = _attention(q, k, vt, k_ctx, vt_ctx, ga, batch, seq)
    out = _out_proj(attn, cv, fix, w_out_bf, x2d, gate, final_norm_g[None, :], seq, ROW_TILE)
    return out.reshape(batch, seq, d)
```
